```python
import math
import jax, jax.numpy as jnp
from jax import lax
import numpy as np

D_MODEL = 1024
BATCH = 2
SEQ = 8192
DEPTH = 1
DEC_BATCH = 16
DEC_SEQ = 4096
PAST_LEN = 128

HEAD_DIM = 64
A_HEADS = 8
A_KV_HEADS = 2
A_GROUP = A_HEADS // A_KV_HEADS
B_HEADS = 4
B_QK_DIM = HEAD_DIM
B_V_DIM = 2 * HEAD_DIM
A_Q_W = A_HEADS * HEAD_DIM
A_KV_W = A_KV_HEADS * HEAD_DIM
B_QK_W = B_HEADS * 2 * B_QK_DIM
B_V_W = B_HEADS * B_V_DIM
IN_W = A_Q_W + 2 * A_KV_W + 2 * B_QK_W + B_V_W
MIX_W = A_HEADS * HEAD_DIM + B_HEADS * B_V_DIM

GRID_W = 64
AXIAL_THETA = 10000.0
ROPE_THETA = 500000.0
PARTIAL_ROT = HEAD_DIM // 4
Q_BLOCK = 128

N_EXPERTS = 32
TOP_K = 4
D_FF = 1024
SWIGLU_LIMIT = 7.0
SWIGLU_ALPHA = 1.702
MOE_BLOCK = 128

EPS = 1e-6

kernel_name = "hymba_style_gqa_diffattn_moe_encoder"


def _rms_norm(x, g):
    xf = x.astype(jnp.float32)
    y = xf * lax.rsqrt(jnp.mean(xf * xf, axis=-1, keepdims=True) + EPS)
    return (y * g.astype(jnp.float32)).astype(x.dtype)


def _rotate_half(x):
    x1, x2 = jnp.split(x, 2, axis=-1)
    return jnp.concatenate([-x2, x1], axis=-1)


def _apply_rope(x, cos, sin):
    c = cos[:, None, :].astype(x.dtype)
    s = sin[:, None, :].astype(x.dtype)
    return x * c + _rotate_half(x) * s


def _axial_cos_sin(seq_len):
    rows = seq_len // GRID_W
    row = jnp.repeat(jnp.arange(rows, dtype=jnp.float32), GRID_W)
    col = jnp.tile(jnp.arange(GRID_W, dtype=jnp.float32), rows)
    half = HEAD_DIM // 2
    inv = AXIAL_THETA ** (-jnp.arange(0, half, 2, dtype=jnp.float32) / half)
    ang = jnp.concatenate([row[:, None] * inv, col[:, None] * inv], axis=-1)
    ang = jnp.concatenate([ang, ang], axis=-1)
    return jnp.cos(ang), jnp.sin(ang)


def _partial_cos_sin(seq_len):
    t = jnp.arange(seq_len, dtype=jnp.float32)
    inv = ROPE_THETA ** (-jnp.arange(0, PARTIAL_ROT, 2, dtype=jnp.float32) / PARTIAL_ROT)
    ang = t[:, None] * inv
    ang = jnp.concatenate([ang, ang], axis=-1)
    return jnp.cos(ang), jnp.sin(ang)


def _apply_partial_rope(x, cos, sin):
    return jnp.concatenate([_apply_rope(x[..., :PARTIAL_ROT], cos, sin), x[..., PARTIAL_ROT:]], axis=-1)


def _to_blocks(q):
    b, s = q.shape[0], q.shape[1]
    q = q.reshape((b, s // Q_BLOCK, Q_BLOCK) + q.shape[2:])
    return jnp.moveaxis(q, 1, 0)


def _from_blocks(o):
    o = jnp.moveaxis(o, 0, 1)
    return o.reshape((o.shape[0], o.shape[1] * o.shape[2]) + o.shape[3:])


def _gqa_attention(q, k, v):
    scale = HEAD_DIM ** -0.5

    def block(qb):
        s = jnp.einsum('bqhgd,bkhd->bhgqk', qb, k).astype(jnp.float32) * scale
        p = jax.nn.softmax(s, axis=-1).astype(v.dtype)
        return jnp.einsum('bhgqk,bkhd->bqhgd', p, v)

    return _from_blocks(lax.map(block, _to_blocks(q)))


def _diff_attention(q1, q2, k1, k2, v, lam):
    scale = B_QK_DIM ** -0.5

    def block(qs):
        qb1, qb2 = qs
        s1 = jnp.einsum('bqhd,bkhd->bhqk', qb1, k1).astype(jnp.float32) * scale
        s2 = jnp.einsum('bqhd,bkhd->bhqk', qb2, k2).astype(jnp.float32) * scale
        p = jax.nn.softmax(s1, axis=-1) - lam * jax.nn.softmax(s2, axis=-1)
        return jnp.einsum('bhqk,bkhe->bqhe', p.astype(v.dtype), v)

    return _from_blocks(lax.map(block, (_to_blocks(q1), _to_blocks(q2))))


def _mixer(n, layer_idx, w_in, a_q_norm, a_k_norm, lq1, lk1, lq2, lk2, b_subln, w_out):
    bsz, s, _ = n.shape
    proj = n @ w_in
    o0 = A_Q_W
    o1 = o0 + A_KV_W
    o2 = o1 + A_KV_W
    o3 = o2 + B_QK_W
    o4 = o3 + B_QK_W
    a_q = proj[..., :o0].reshape(bsz, s, A_HEADS, HEAD_DIM)
    a_k = proj[..., o0:o1].reshape(bsz, s, A_KV_HEADS, HEAD_DIM)
    a_v = proj[..., o1:o2].reshape(bsz, s, A_KV_HEADS, HEAD_DIM)
    b_q = proj[..., o2:o3].reshape(bsz, s, B_HEADS, 2, B_QK_DIM)
    b_k = proj[..., o3:o4].reshape(bsz, s, B_HEADS, 2, B_QK_DIM)
    b_v = proj[..., o4:].reshape(bsz, s, B_HEADS, B_V_DIM)

    cos_a, sin_a = _axial_cos_sin(s)
    a_q = _apply_rope(_rms_norm(a_q, a_q_norm), cos_a, sin_a)
    a_k = _apply_rope(_rms_norm(a_k, a_k_norm), cos_a, sin_a)
    a_q = a_q.reshape(bsz, s, A_KV_HEADS, A_GROUP, HEAD_DIM)
    out_a = _gqa_attention(a_q, a_k, a_v).reshape(bsz, s, A_HEADS * HEAD_DIM)

    cos_b, sin_b = _partial_cos_sin(s)
    q1 = _apply_partial_rope(b_q[..., 0, :], cos_b, sin_b)
    q2 = _apply_partial_rope(b_q[..., 1, :], cos_b, sin_b)
    k1 = _apply_partial_rope(b_k[..., 0, :], cos_b, sin_b)
    k2 = _apply_partial_rope(b_k[..., 1, :], cos_b, sin_b)
    lambda_init = 0.8 - 0.6 * math.exp(-0.3 * layer_idx)
    lam = (jnp.exp(jnp.sum(lq1.astype(jnp.float32) * lk1.astype(jnp.float32)))
           - jnp.exp(jnp.sum(lq2.astype(jnp.float32) * lk2.astype(jnp.float32)))
           + lambda_init)
    out_b = _diff_attention(q1, q2, k1, k2, b_v, lam)
    out_b = _rms_norm(out_b, b_subln) * (1.0 - lambda_init)
    out_b = out_b.reshape(bsz, s, B_HEADS * B_V_DIM)

    return jnp.concatenate([out_a, out_b], axis=-1) @ w_out


def _moe(x, router_w, router_b, w_gate, b_gate, w_up, b_up, w_down, b_down):
    bsz, s, d = x.shape
    xf = x.reshape(-1, d)
    t = xf.shape[0]
    a = t * TOP_K
    logits = (xf @ router_w + router_b).astype(jnp.float32)
    top_v, top_e = lax.top_k(logits, TOP_K)
    gates = jax.nn.softmax(top_v, axis=-1)

    flat_e = top_e.reshape(-1)
    order = jnp.argsort(flat_e)
    sorted_e = flat_e[order]
    tok = order // TOP_K
    counts = jnp.bincount(flat_e, length=N_EXPERTS)
    padded = ((counts + MOE_BLOCK - 1) // MOE_BLOCK) * MOE_BLOCK
    start = jnp.cumsum(counts) - counts
    pend = jnp.cumsum(padded)
    pstart = pend - padded
    dest = pstart[sorted_e] + (jnp.arange(a) - start[sorted_e])
    n_blocks = -(-a // MOE_BLOCK) + N_EXPERTS
    p_rows = n_blocks * MOE_BLOCK
    xpad = jnp.zeros((p_rows, d), x.dtype).at[dest].set(xf[tok])
    block_e = jnp.minimum(
        jnp.searchsorted(pend, jnp.arange(n_blocks) * MOE_BLOCK, side='right'), N_EXPERTS - 1)

    def expert_block(args):
        xb, e = args
        g = xb @ w_gate[e] + b_gate[e]
        u = xb @ w_up[e] + b_up[e]
        g = jnp.minimum(g, SWIGLU_LIMIT)
        u = jnp.clip(u, -SWIGLU_LIMIT, SWIGLU_LIMIT)
        h = g * jax.nn.sigmoid(g * SWIGLU_ALPHA) * (u + 1.0)
        return h @ w_down[e] + b_down[e]

    ypad = lax.map(expert_block, (xpad.reshape(n_blocks, MOE_BLOCK, d), block_e)).reshape(p_rows, d)
    y = jnp.zeros((a, d), ypad.dtype).at[order].set(ypad[dest]).reshape(t, TOP_K, d)
    out = jnp.einsum('tkd,tk->td', y, gates.astype(y.dtype))
    return out.reshape(bsz, s, d)


def _trunk(x, attn_norm, w_in, a_q_norm, a_k_norm, b_lambda_q1, b_lambda_k1, b_lambda_q2,
           b_lambda_k2, b_subln, w_out, ffn_norm, router_w, router_b, w_gate, b_gate, w_up,
           b_up, w_down, b_down, final_norm):
    for l in range(DEPTH):
        n = _rms_norm(x, attn_norm[l])
        x = x + _mixer(n, l, w_in[l], a_q_norm[l], a_k_norm[l], b_lambda_q1[l], b_lambda_k1[l],
                       b_lambda_q2[l], b_lambda_k2[l], b_subln[l], w_out[l])
        x = x + _moe(_rms_norm(x, ffn_norm[l]), router_w[l], router_b[l], w_gate[l], b_gate[l],
                     w_up[l], b_up[l], w_down[l], b_down[l])
    return _rms_norm(x, final_norm)


def setup_inputs(seed: int = 0) -> dict:
    key = jax.random.key(seed)
    ks = jax.random.split(key, 24)
    f32 = jnp.float32
    nrm = lambda k, shape, scale: jax.random.normal(k, shape, f32) * scale
    return {
        "x_prompt": nrm(ks[0], (BATCH, SEQ, D_MODEL), 1.0),
        "x_sample": nrm(ks[1], (DEC_BATCH, DEC_SEQ, D_MODEL), 1.0),
        "attn_norm": 1.0 + nrm(ks[2], (DEPTH, D_MODEL), 0.02),
        "w_in": nrm(ks[3], (DEPTH, D_MODEL, IN_W), D_MODEL ** -0.5),
        "a_q_norm": 1.0 + nrm(ks[4], (DEPTH, HEAD_DIM), 0.02),
        "a_k_norm": 1.0 + nrm(ks[5], (DEPTH, HEAD_DIM), 0.02),
        "b_lambda_q1": nrm(ks[6], (DEPTH, B_QK_DIM), 0.1),
        "b_lambda_k1": nrm(ks[7], (DEPTH, B_QK_DIM), 0.1),
        "b_lambda_q2": nrm(ks[8], (DEPTH, B_QK_DIM), 0.1),
        "b_lambda_k2": nrm(ks[9], (DEPTH, B_QK_DIM), 0.1),
        "b_subln": 1.0 + nrm(ks[10], (DEPTH, B_V_DIM), 0.02),
        "w_out": nrm(ks[11], (DEPTH, MIX_W, D_MODEL), MIX_W ** -0.5),
        "ffn_norm": 1.0 + nrm(ks[12], (DEPTH, D_MODEL), 0.02),
        "router_w": nrm(ks[13], (DEPTH, D_MODEL, N_EXPERTS), D_MODEL ** -0.5),
        "router_b": nrm(ks[14], (DEPTH, N_EXPERTS), 0.01),
        "w_gate": nrm(ks[15], (DEPTH, N_EXPERTS, D_MODEL, D_FF), D_MODEL ** -0.5),
        "b_gate": nrm(ks[16], (DEPTH, N_EXPERTS, D_FF), 0.01),
        "w_up": nrm(ks[17], (DEPTH, N_EXPERTS, D_MODEL, D_FF), D_MODEL ** -0.5),
        "b_up": nrm(ks[18], (DEPTH, N_EXPERTS, D_FF), 0.01),
        "w_down": nrm(ks[19], (DEPTH, N_EXPERTS, D_FF, D_MODEL), D_FF ** -0.5),
        "b_down": nrm(ks[20], (DEPTH, N_EXPERTS, D_MODEL), 0.01),
        "final_norm": 1.0 + nrm(ks[21], (D_MODEL,), 0.02),
    }


def reference(x_prompt, x_sample, attn_norm, w_in, a_q_norm, a_k_norm, b_lambda_q1, b_lambda_k1,
              b_lambda_q2, b_lambda_k2, b_subln, w_out, ffn_norm, router_w, router_b, w_gate,
              b_gate, w_up, b_up, w_down, b_down, final_norm):
    y_prompt = _trunk(x_prompt, attn_norm, w_in, a_q_norm, a_k_norm, b_lambda_q1, b_lambda_k1,
                      b_lambda_q2, b_lambda_k2, b_subln, w_out, ffn_norm, router_w, router_b,
                      w_gate, b_gate, w_up, b_up, w_down, b_down, final_norm)
    y_sample = _trunk(x_sample, attn_norm, w_in, a_q_norm, a_k_norm, b_lambda_q1, b_lambda_k1,
                      b_lambda_q2, b_lambda_k2, b_subln, w_out, ffn_norm, router_w, router_b,
                      w_gate, b_gate, w_up, b_up, w_down, b_down, final_norm)
    return (y_prompt, y_sample)
```

```python
import functools
import math

import jax
import jax.numpy as jnp
from jax import lax
from jax.experimental import pallas as pl
from jax.experimental.pallas import tpu as pltpu

F32 = jnp.float32
BF16 = jnp.bfloat16

D_MODEL = 1024
HEAD_DIM = 64
A_HEADS = 8
A_KV_HEADS = 2
A_GROUP = A_HEADS // A_KV_HEADS
B_HEADS = 4
B_V_DIM = 2 * HEAD_DIM
A_Q_W = A_HEADS * HEAD_DIM
A_KV_W = A_KV_HEADS * HEAD_DIM
B_QK_W = B_HEADS * 2 * HEAD_DIM
B_V_W = B_HEADS * B_V_DIM
IN_W = A_Q_W + 2 * A_KV_W + 2 * B_QK_W + B_V_W
GRID_W = 64
AXIAL_THETA = 10000.0
ROPE_THETA = 500000.0
PARTIAL_ROT = HEAD_DIM // 4
N_EXPERTS = 32
TOP_K = 4
SWIGLU_LIMIT = 7.0
SWIGLU_ALPHA = 1.702
EPS = 1e-6
LAMBDA_INIT = 0.8 - 0.6 * math.exp(-0.3 * 0)
QK_SCALE = HEAD_DIM ** -0.5

O_AQ = 0
O_AK = O_AQ + A_Q_W
O_AV = O_AK + A_KV_W
O_BQ = O_AV + A_KV_W
O_BK = O_BQ + B_QK_W
O_BV = O_BK + B_QK_W

R_QA = 0
R_VA = R_QA + A_Q_W
R_QB = R_VA + A_KV_W
R_VB = R_QB + B_QK_W
QV_ROWS = R_VB + B_V_W

LANES = 128
TOKEN_TILE = 512
A_Q_BLOCK = 128
B_Q_BLOCK = 512
KV_CHUNK = 512
MOE_ROWS = 256
COMBINE_TOKENS = 128
VMEM_LIMIT = 48 * 1024 * 1024


def _cparams(sem):
    return pltpu.CompilerParams(dimension_semantics=sem, vmem_limit_bytes=VMEM_LIMIT)


def _inproj_kernel(x_ref, g_ref, w_ref, cosa_ref, sina_ref, cosb_ref, sinb_ref, qg_ref, kg_ref,
                   qv_ref, ka_ref, kb_ref):
    x = x_ref[0]
    ms = jnp.mean(x * x, axis=-1, keepdims=True)
    n = (x * lax.rsqrt(ms + EPS) * g_ref[...]).astype(BF16)
    pt = lax.dot_general(w_ref[...], n, (((1,), (1,)), ((), ())), preferred_element_type=F32)

    cosa = cosa_ref[...]
    sina = sina_ref[...]
    cosb = cosb_ref[...]
    sinb = sinb_ref[...]
    half = HEAD_DIM // 2
    hrot = PARTIAL_ROT // 2

    def norm_rope_a(t, gain):
        y = t * lax.rsqrt(jnp.mean(t * t, axis=0, keepdims=True) + EPS) * gain
        rot = jnp.concatenate([y[half:], y[:half]], axis=0)
        return y * cosa + rot * sina

    def rope_b(t):
        head = t[:PARTIAL_ROT]
        rot = jnp.concatenate([head[hrot:], head[:hrot]], axis=0)
        return jnp.concatenate([head * cosb + rot * sinb, t[PARTIAL_ROT:]], axis=0)

    qg = qg_ref[...]
    kg = kg_ref[...]
    for h in range(A_HEADS):
        t = pt[O_AQ + h * HEAD_DIM:O_AQ + (h + 1) * HEAD_DIM]
        qv_ref[0, R_QA + h * HEAD_DIM:R_QA + (h + 1) * HEAD_DIM, :] = (
            norm_rope_a(t, qg) * QK_SCALE).astype(BF16)
    qv_ref[0, R_VA:R_VA + A_KV_W, :] = pt[O_AV:O_AV + A_KV_W].astype(BF16)
    for j in range(B_QK_W // HEAD_DIM):
        t = pt[O_BQ + j * HEAD_DIM:O_BQ + (j + 1) * HEAD_DIM]
        qv_ref[0, R_QB + j * HEAD_DIM:R_QB + (j + 1) * HEAD_DIM, :] = (rope_b(t) * QK_SCALE).astype(BF16)
    qv_ref[0, R_VB:R_VB + B_V_W, :] = pt[O_BV:O_BV + B_V_W].astype(BF16)

    ka = jnp.concatenate([norm_rope_a(pt[O_AK + h * HEAD_DIM:O_AK + (h + 1) * HEAD_DIM], kg)
                          for h in range(A_KV_HEADS)], axis=0)
    ka_ref[0] = ka.T.astype(BF16)
    for h in range(B_HEADS):
        kb = jnp.concatenate([rope_b(pt[O_BK + (2 * h + c) * HEAD_DIM:O_BK + (2 * h + c + 1) * HEAD_DIM])
                              for c in range(2)], axis=0)
        kb_ref[0, h] = kb.T.astype(BF16)


def _in_projection(x, g_attn, w_in_t, tabs, qg, kg):
    bsz, s, d = x.shape
    tm = TOKEN_TILE
    cosa, sina, cosb, sinb = tabs
    const = lambda shape: pl.BlockSpec(shape, lambda b, i: (0,) * len(shape))
    return pl.pallas_call(
        _inproj_kernel,
        grid=(bsz, s // tm),
        in_specs=[
            pl.BlockSpec((1, tm, d), lambda b, i: (b, i, 0)),
            const((1, d)),
            const((IN_W, d)),
            pl.BlockSpec((HEAD_DIM, tm), lambda b, i: (0, i)),
            pl.BlockSpec((HEAD_DIM, tm), lambda b, i: (0, i)),
            pl.BlockSpec((PARTIAL_ROT, tm), lambda b, i: (0, i)),
            pl.BlockSpec((PARTIAL_ROT, tm), lambda b, i: (0, i)),
            const((HEAD_DIM, tm)),
            const((HEAD_DIM, tm)),
        ],
        out_specs=[
            pl.BlockSpec((1, QV_ROWS, tm), lambda b, i: (b, 0, i)),
            pl.BlockSpec((1, tm, LANES), lambda b, i: (b, i, 0)),
            pl.BlockSpec((1, B_HEADS, tm, LANES), lambda b, i: (b, 0, i, 0)),
        ],
        out_shape=[
            jax.ShapeDtypeStruct((bsz, QV_ROWS, s), BF16),
            jax.ShapeDtypeStruct((bsz, s, LANES), BF16),
            jax.ShapeDtypeStruct((bsz, B_HEADS, s, LANES), BF16),
        ],
        compiler_params=_cparams(("parallel", "parallel")),
        name="in_projection",
    )(x, g_attn, w_in_t, cosa, sina, cosb, sinb, qg, kg)


def _column_softmax_attention(k_ref, rhs, v_ref, splits, seq):
    n = rhs.shape[1]
    bk = KV_CHUNK

    def body(c, carry):
        m, l, accs = carry
        off = pl.multiple_of(c * bk, bk)
        s = jnp.dot(k_ref[pl.ds(off, bk), :], rhs, preferred_element_type=F32)
        m_new = jnp.maximum(m, jnp.max(s, axis=0, keepdims=True))
        alpha = jnp.exp(m - m_new)
        p = jnp.exp(s - m_new)
        l = alpha * l + jnp.sum(p, axis=0, keepdims=True)
        pb = p.astype(BF16)
        new = []
        for (rl, rh, cl, ch), acc in zip(splits, accs):
            pv = jnp.dot(v_ref[rl:rh, pl.ds(off, bk)], pb[:, cl:ch], preferred_element_type=F32)
            new.append(alpha[:, cl:ch] * acc + pv)
        return m_new, l, tuple(new)

    init = (jnp.full((1, n), -jnp.inf, F32), jnp.zeros((1, n), F32),
            tuple(jnp.zeros((rh - rl, ch - cl), F32) for rl, rh, cl, ch in splits))
    _, l, accs = lax.fori_loop(0, seq // bk, body, init)
    return l, accs


def _attn_a_kernel(q_ref, k_ref, v_ref, o_ref, *, seq):
    bq = A_Q_BLOCK
    q = q_ref[0]
    half_n = A_GROUP * bq
    zeros = jnp.zeros((HEAD_DIM, half_n), BF16)
    heads = [q[h * HEAD_DIM:(h + 1) * HEAD_DIM] for h in range(A_HEADS)]
    top = jnp.concatenate(heads[:A_GROUP] + [zeros], axis=1)
    bot = jnp.concatenate([zeros] + heads[A_GROUP:], axis=1)
    rhs = jnp.concatenate([top, bot], axis=0)
    splits = [(0, HEAD_DIM, 0, half_n), (HEAD_DIM, 2 * HEAD_DIM, half_n, 2 * half_n)]
    l, accs = _column_softmax_attention(k_ref.at[0], rhs, v_ref.at[0], splits, seq)
    inv = 1.0 / l
    for kv in range(A_KV_HEADS):
        o = accs[kv] * inv[:, kv * half_n:(kv + 1) * half_n]
        for g in range(A_GROUP):
            h = kv * A_GROUP + g
            o_ref[0, h * HEAD_DIM:(h + 1) * HEAD_DIM, :] = o[:, g * bq:(g + 1) * bq].astype(BF16)


def _attention_a(qv, ka):
    bsz, _, s = qv.shape
    bq = A_Q_BLOCK
    return pl.pallas_call(
        functools.partial(_attn_a_kernel, seq=s),
        grid=(bsz, s // bq),
        in_specs=[
            pl.BlockSpec((1, A_Q_W, bq), lambda b, i: (b, R_QA // A_Q_W, i)),
            pl.BlockSpec((1, s, LANES), lambda b, i: (b, 0, 0)),
            pl.BlockSpec((1, A_KV_W, s), lambda b, i: (b, R_VA // A_KV_W, 0)),
        ],
        out_specs=pl.BlockSpec((1, A_Q_W, bq), lambda b, i: (b, 0, i)),
        out_shape=jax.ShapeDtypeStruct((bsz, A_Q_W, s), BF16),
        compiler_params=_cparams(("parallel", "parallel")),
        name="attention_a",
    )(qv, ka, qv)


def _attn_b_kernel(lam_ref, q_ref, k_ref, v_ref, sg_ref, o_ref, *, seq):
    bq = B_Q_BLOCK
    q = q_ref[0]
    zeros = jnp.zeros((HEAD_DIM, bq), BF16)
    rhs = jnp.concatenate([jnp.concatenate([q[:HEAD_DIM], zeros], axis=1),
                           jnp.concatenate([zeros, q[HEAD_DIM:]], axis=1)], axis=0)
    splits = [(0, B_V_DIM, 0, 2 * bq)]
    l, accs = _column_softmax_attention(k_ref.at[0, 0], rhs, v_ref.at[0], splits, seq)
    o = accs[0] * (1.0 / l)
    o = o[:, :bq] - lam_ref[0] * o[:, bq:]
    o = o * lax.rsqrt(jnp.mean(o * o, axis=0, keepdims=True) + EPS) * sg_ref[...]
    o_ref[0] = (o * (1.0 - LAMBDA_INIT)).astype(BF16)


def _attention_b(lam, qv, kb, subln):
    bsz, _, s = qv.shape
    bq = B_Q_BLOCK
    return pl.pallas_call(
        functools.partial(_attn_b_kernel, seq=s),
        grid=(bsz, B_HEADS, s // bq),
        in_specs=[
            pl.BlockSpec(memory_space=pltpu.SMEM),
            pl.BlockSpec((1, B_V_DIM, bq), lambda b, h, i: (b, R_QB // B_V_DIM + h, i)),
            pl.BlockSpec((1, 1, s, LANES), lambda b, h, i: (b, h, 0, 0)),
            pl.BlockSpec((1, B_V_DIM, s), lambda b, h, i: (b, R_VB // B_V_DIM + h, 0)),
            pl.BlockSpec((B_V_DIM, bq), lambda b, h, i: (0, 0)),
        ],
        out_specs=pl.BlockSpec((1, B_V_DIM, bq), lambda b, h, i: (b, h, i)),
        out_shape=jax.ShapeDtypeStruct((bsz, B_V_W, s), BF16),
        compiler_params=_cparams(("parallel", "parallel", "parallel")),
        name="attention_b",
    )(lam, qv, kb, qv, subln)


def _outproj_kernel(a_ref, b_ref, x_ref, wa_ref, wb_ref, g_ref, rw_ref, rb_ref,
                    x2_ref, xn_ref, te_ref, tg_ref):
    tn = (((0,), (0,)), ((), ()))
    y = (x_ref[0]
         + lax.dot_general(a_ref[0], wa_ref[...], tn, preferred_element_type=F32)
         + lax.dot_general(b_ref[0], wb_ref[...], tn, preferred_element_type=F32))
    x2_ref[0] = y
    xn = y * lax.rsqrt(jnp.mean(y * y, axis=-1, keepdims=True) + EPS) * g_ref[...]
    xn_ref[0] = xn
    logits = lax.dot_general(rw_ref[...], xn, (((1,), (1,)), ((), ())),
                             precision=lax.Precision.HIGHEST, preferred_element_type=F32) + rb_ref[...]
    iota = lax.broadcasted_iota(jnp.int32, logits.shape, 0)
    work = logits
    vals, idxs = [], []
    for _ in range(TOP_K):
        mx = jnp.max(work, axis=0, keepdims=True)
        idx = jnp.min(jnp.where(work == mx, iota, N_EXPERTS), axis=0, keepdims=True)
        vals.append(mx)
        idxs.append(idx)
        work = jnp.where(iota == idx, -jnp.inf, work)
    ex = [jnp.exp(v - vals[0]) for v in vals]
    inv = 1.0 / (ex[0] + ex[1] + ex[2] + ex[3])
    te_ref[0] = jnp.concatenate(idxs, axis=0)
    tg_ref[0] = jnp.concatenate([e * inv for e in ex], axis=0)


def _out_projection(oa, ob, x, wa, wb, g_ffn, rw_t, rb):
    bsz, s, d = x.shape
    tm = TOKEN_TILE
    const = lambda shape: pl.BlockSpec(shape, lambda b, i: (0,) * len(shape))
    tok = pl.BlockSpec((1, tm, d), lambda b, i: (b, i, 0))
    sel = pl.BlockSpec((1, TOP_K, tm), lambda b, i: (b, 0, i))
    return pl.pallas_call(
        _outproj_kernel,
        grid=(bsz, s // tm),
        in_specs=[
            pl.BlockSpec((1, A_Q_W, tm), lambda b, i: (b, 0, i)),
            pl.BlockSpec((1, B_V_W, tm), lambda b, i: (b, 0, i)),
            tok,
            const((A_Q_W, d)),
            const((B_V_W, d)),
            const((1, d)),
            const((N_EXPERTS, d)),
            const((N_EXPERTS, tm)),
        ],
        out_specs=[tok, tok, sel, sel],
        out_shape=[
            jax.ShapeDtypeStruct((bsz, s, d), F32),
            jax.ShapeDtypeStruct((bsz, s, d), F32),
            jax.ShapeDtypeStruct((bsz, TOP_K, s), jnp.int32),
            jax.ShapeDtypeStruct((bsz, TOP_K, s), F32),
        ],
        compiler_params=_cparams(("parallel", "parallel")),
        name="out_projection_router",
    )(oa, ob, x, wa, wb, g_ffn, rw_t, rb)


def _start_row_gather(idx_ref, src_hbm, buf, sem, slot, rows):
    def body(r, carry):
        pltpu.make_async_copy(src_hbm.at[pl.ds(idx_ref[0, 0, r], 1), :],
                              buf.at[slot, pl.ds(r, 1), :], sem.at[slot]).start()
        return carry
    lax.fori_loop(0, rows, body, 0, unroll=8)


def _wait_row_gather(src_hbm, buf, sem, slot, rows):
    pltpu.make_async_copy(src_hbm.at[pl.ds(0, rows), :], buf.at[slot], sem.at[slot]).wait()


def _pipelined_gather(step, n_steps, idx_cur, idx_nxt, src_hbm, buf, sem, rows):
    slot = lax.rem(step, 2)

    @pl.when(jnp.logical_and(step == 0, n_steps > 0))
    def _():
        _start_row_gather(idx_cur, src_hbm, buf, sem, 0, rows)

    @pl.when(step + 1 < n_steps)
    def _():
        _start_row_gather(idx_nxt, src_hbm, buf, sem, 1 - slot, rows)

    @pl.when(step < n_steps)
    def _():
        _wait_row_gather(src_hbm, buf, sem, slot, rows)

    return slot


def _moe_kernel(be_ref, nb_ref, tok_cur, tok_nxt, xn_hbm, wg_ref, bg_ref, wu_ref, bu_ref, wd_ref, bd_ref,
                y_ref, buf, sem):
    i = pl.program_id(0)
    n_used = nb_ref[0]
    slot = _pipelined_gather(i, n_used, tok_cur, tok_nxt, xn_hbm, buf, sem, MOE_ROWS)

    @pl.when(i < n_used)
    def _():
        xb = buf[slot].astype(BF16)
        g = jnp.dot(xb, wg_ref[0], preferred_element_type=F32) + bg_ref[0]
        u = jnp.dot(xb, wu_ref[0], preferred_element_type=F32) + bu_ref[0]
        g = jnp.minimum(g, SWIGLU_LIMIT)
        u = jnp.clip(u, -SWIGLU_LIMIT, SWIGLU_LIMIT)
        h = g * jax.nn.sigmoid(g * SWIGLU_ALPHA) * (u + 1.0)
        y_ref[...] = jnp.dot(h.astype(BF16), wd_ref[0], preferred_element_type=F32) + bd_ref[0]

    @pl.when(i >= n_used)
    def _():
        y_ref[...] = jnp.zeros_like(y_ref)


def _moe_experts(block_e, n_used, src_tok, xn, wg, bg, wu, bu, wd, bd):
    t, d = xn.shape
    bm = MOE_ROWS
    n_blocks = src_tok.shape[0]
    dff = wg.shape[2]
    last = n_blocks - 1
    wspec = lambda k, n: pl.BlockSpec((1, k, n), lambda i, be, nb: (be[i], 0, 0))
    grid_spec = pltpu.PrefetchScalarGridSpec(
        num_scalar_prefetch=2,
        grid=(n_blocks,),
        in_specs=[
            pl.BlockSpec((1, 1, bm), lambda i, be, nb: (i, 0, 0), memory_space=pltpu.SMEM),
            pl.BlockSpec((1, 1, bm), lambda i, be, nb: (jnp.minimum(i + 1, last), 0, 0),
                         memory_space=pltpu.SMEM),
            pl.BlockSpec(memory_space=pl.ANY),
            wspec(d, dff), wspec(1, dff), wspec(d, dff), wspec(1, dff), wspec(dff, d), wspec(1, d),
        ],
        out_specs=pl.BlockSpec((bm, d), lambda i, be, nb: (i, 0)),
        scratch_shapes=[pltpu.VMEM((2, bm, d), F32), pltpu.SemaphoreType.DMA((2,))],
    )
    return pl.pallas_call(
        _moe_kernel,
        grid_spec=grid_spec,
        out_shape=jax.ShapeDtypeStruct((n_blocks * bm, d), F32),
        compiler_params=_cparams(("arbitrary",)),
        name="moe_experts",
    )(block_e, n_used, src_tok, src_tok, xn, wg, bg, wu, bu, wd, bd)


def _combine_kernel(dst_cur, dst_nxt, y_hbm, x2_ref, gate_ref, g_ref, o_ref, buf, sem):
    i = pl.program_id(0)
    tc = COMBINE_TOKENS
    slot = _pipelined_gather(i, pl.num_programs(0), dst_cur, dst_nxt, y_hbm, buf, sem, TOP_K * tc)
    gates = gate_ref[...]
    y = x2_ref[...]
    for k in range(TOP_K):
        y = y + gates[:, k:k + 1] * buf[slot, k * tc:(k + 1) * tc, :]
    o_ref[...] = y * lax.rsqrt(jnp.mean(y * y, axis=-1, keepdims=True) + EPS) * g_ref[...]


def _combine(dest, ypad, x2, gates, g_final):
    t, d = x2.shape
    tc = COMBINE_TOKENS
    n_steps = t // tc
    last = n_steps - 1
    return pl.pallas_call(
        _combine_kernel,
        grid=(n_steps,),
        in_specs=[
            pl.BlockSpec((1, 1, TOP_K * tc), lambda i: (i, 0, 0), memory_space=pltpu.SMEM),
            pl.BlockSpec((1, 1, TOP_K * tc), lambda i: (jnp.minimum(i + 1, last), 0, 0),
                         memory_space=pltpu.SMEM),
            pl.BlockSpec(memory_space=pl.ANY),
            pl.BlockSpec((tc, d), lambda i: (i, 0)),
            pl.BlockSpec((tc, TOP_K), lambda i: (i, 0)),
            pl.BlockSpec((1, d), lambda i: (0, 0)),
        ],
        out_specs=pl.BlockSpec((tc, d), lambda i: (i, 0)),
        out_shape=jax.ShapeDtypeStruct((t, d), F32),
        scratch_shapes=[pltpu.VMEM((2, TOP_K * tc, d), F32), pltpu.SemaphoreType.DMA((2,))],
        compiler_params=_cparams(("arbitrary",)),
        name="combine_final_norm",
    )(dest, dest, ypad, x2, gates, g_final)


def _route(top_e):
    t = top_e.shape[0]
    a = t * TOP_K
    bm = MOE_ROWS
    n_blocks = a // bm + N_EXPERTS
    flat_e = top_e.reshape(-1)
    order = jnp.argsort(flat_e, stable=True).astype(jnp.int32)
    rank = jnp.argsort(order).astype(jnp.int32)
    counts = jnp.bincount(flat_e, length=N_EXPERTS).astype(jnp.int32)
    padded = ((counts + bm - 1) // bm) * bm
    start = jnp.cumsum(counts) - counts
    pend = jnp.cumsum(padded)
    pstart = pend - padded
    dest = pstart[flat_e] + rank - start[flat_e]
    block_e = jnp.minimum(jnp.searchsorted(pend, jnp.arange(n_blocks, dtype=jnp.int32) * bm, side='right'),
                          N_EXPERTS - 1).astype(jnp.int32)
    rows = jnp.arange(n_blocks * bm, dtype=jnp.int32)
    row_e = jnp.repeat(block_e, bm)
    within = rows - pstart[row_e]
    valid = within < counts[row_e]
    src = jnp.where(valid, order[jnp.clip(start[row_e] + within, 0, a - 1)] // TOP_K, 0)
    n_used = (pend[-1] // bm).astype(jnp.int32).reshape(1)
    return block_e, n_used, src.astype(jnp.int32).reshape(n_blocks, 1, bm), dest.astype(jnp.int32)


def _rope_tables(seq, tile):
    rows = seq // GRID_W
    row = jnp.repeat(jnp.arange(rows, dtype=F32), GRID_W)
    col = jnp.tile(jnp.arange(GRID_W, dtype=F32), rows)
    half = HEAD_DIM // 2
    inv = AXIAL_THETA ** (-jnp.arange(0, half, 2, dtype=F32) / half)
    ang = jnp.concatenate([row[:, None] * inv, col[:, None] * inv], axis=-1)
    ang = jnp.concatenate([ang, ang], axis=-1).T
    sign_a = jnp.where(jnp.arange(HEAD_DIM) < half, -1.0, 1.0).astype(F32)[:, None]
    tt = jnp.arange(seq, dtype=F32)
    invb = ROPE_THETA ** (-jnp.arange(0, PARTIAL_ROT, 2, dtype=F32) / PARTIAL_ROT)
    angb = tt[:, None] * invb
    angb = jnp.concatenate([angb, angb], axis=-1).T
    sign_b = jnp.where(jnp.arange(PARTIAL_ROT) < PARTIAL_ROT // 2, -1.0, 1.0).astype(F32)[:, None]
    return jnp.cos(ang), jnp.sin(ang) * sign_a, jnp.cos(angb), jnp.sin(angb) * sign_b


def _trunk(x, p):
    bsz, s, d = x.shape
    t = bsz * s
    tabs = _rope_tables(s, TOKEN_TILE)
    qv, ka, kb = _in_projection(x, p["g_attn"], p["w_in_t"], tabs, p["qg"], p["kg"])
    oa = _attention_a(qv, ka)
    ob = _attention_b(p["lam"], qv, kb, p["subln"])
    x2, xn, te, tg = _out_projection(oa, ob, x, p["wa"], p["wb"], p["g_ffn"], p["rw_t"], p["rb"])
    top_e = jnp.swapaxes(te, 1, 2).reshape(t, TOP_K)
    gates = jnp.swapaxes(tg, 1, 2).reshape(t, TOP_K)
    block_e, n_used, src_tok, dest = _route(top_e)
    ypad = _moe_experts(block_e, n_used, src_tok, xn.reshape(t, d),
                        p["wg"], p["bg"], p["wu"], p["bu"], p["wd"], p["bd"])
    tc = COMBINE_TOKENS
    dest_tiles = jnp.swapaxes(dest.reshape(t // tc, tc, TOP_K), 1, 2).reshape(t // tc, 1, TOP_K * tc)
    out = _combine(dest_tiles, ypad, x2.reshape(t, d), gates, p["g_final"])
    return out.reshape(bsz, s, d)


def kernel(x_prompt, x_sample, attn_norm, w_in, a_q_norm, a_k_norm, b_lambda_q1, b_lambda_k1, b_lambda_q2,
           b_lambda_k2, b_subln, w_out, ffn_norm, router_w, router_b, w_gate, b_gate, w_up, b_up, w_down,
           b_down, final_norm):
    tm = TOKEN_TILE
    lam = (jnp.exp(jnp.sum(b_lambda_q1[0].astype(F32) * b_lambda_k1[0].astype(F32)))
           - jnp.exp(jnp.sum(b_lambda_q2[0].astype(F32) * b_lambda_k2[0].astype(F32)))
           + LAMBDA_INIT)
    p = {
        "g_attn": attn_norm[0].reshape(1, D_MODEL),
        "w_in_t": w_in[0].T.astype(BF16),
        "qg": jnp.broadcast_to(a_q_norm[0][:, None], (HEAD_DIM, tm)),
        "kg": jnp.broadcast_to(a_k_norm[0][:, None], (HEAD_DIM, tm)),
        "lam": lam.reshape(1).astype(F32),
        "subln": jnp.broadcast_to(b_subln[0][:, None], (B_V_DIM, B_Q_BLOCK)),
        "wa": w_out[0, :A_Q_W].astype(BF16),
        "wb": w_out[0, A_Q_W:].astype(BF16),
        "g_ffn": ffn_norm[0].reshape(1, D_MODEL),
        "rw_t": router_w[0].T,
        "rb": jnp.broadcast_to(router_b[0][:, None], (N_EXPERTS, tm)),
        "wg": w_gate[0].astype(BF16),
        "bg": b_gate[0].reshape(N_EXPERTS, 1, -1),
        "wu": w_up[0].astype(BF16),
        "bu": b_up[0].reshape(N_EXPERTS, 1, -1),
        "wd": w_down[0].astype(BF16),
        "bd": b_down[0].reshape(N_EXPERTS, 1, -1),
        "g_final": final_norm.reshape(1, D_MODEL),
    }
    return _trunk(x_prompt, p), _trunk(x_sample, p)
```

```python
import functools
import math

import jax
import jax.numpy as jnp
from jax import lax
from jax.experimental import pallas as pl
from jax.experimental.pallas import tpu as pltpu

F32 = jnp.float32
BF16 = jnp.bfloat16

D_MODEL = 1024
HEAD_DIM = 64
A_HEADS = 8
A_KV_HEADS = 2
A_GROUP = A_HEADS // A_KV_HEADS
B_HEADS = 4
B_V_DIM = 2 * HEAD_DIM
A_Q_W = A_HEADS * HEAD_DIM
A_KV_W = A_KV_HEADS * HEAD_DIM
B_QK_W = B_HEADS * 2 * HEAD_DIM
B_V_W = B_HEADS * B_V_DIM
IN_W = A_Q_W + 2 * A_KV_W + 2 * B_QK_W + B_V_W
GRID_W = 64
AXIAL_THETA = 10000.0
ROPE_THETA = 500000.0
PARTIAL_ROT = HEAD_DIM // 4
N_EXPERTS = 32
TOP_K = 4
SWIGLU_LIMIT = 7.0
SWIGLU_ALPHA = 1.702
EPS = 1e-6
LAMBDA_INIT = 0.8 - 0.6 * math.exp(-0.3 * 0)
Q_PRESCALE = HEAD_DIM ** -0.5 * math.log2(math.e)

O_AQ = 0
O_AK = O_AQ + A_Q_W
O_AV = O_AK + A_KV_W
O_BQ = O_AV + A_KV_W
O_BK = O_BQ + B_QK_W
O_BV = O_BK + B_QK_W

LANES = 128
BF16_SUBLANES = 16
ONES_ROWS = BF16_SUBLANES
VA_ROWS = HEAD_DIM + ONES_ROWS
VB_ROWS = B_V_DIM + ONES_ROWS
TOKEN_TILE = 512
A_Q_BLOCK = 128
B_Q_BLOCK = 512
KV_CHUNK = 256
COL_GROUP = 256
MOE_ROWS = 256
COMBINE_TOKENS = 256
VMEM_LIMIT = 48 * 1024 * 1024


def _cparams(sem):
    return pltpu.CompilerParams(dimension_semantics=sem, vmem_limit_bytes=VMEM_LIMIT)


def _inproj_kernel(x_ref, g_ref, w_ref, cosa_ref, sina_ref, cosb_ref, sinb_ref, qg_ref, kg_ref,
                   qa_ref, qb_ref, va_ref, vb_ref, ka_ref, kb_ref):
    x = x_ref[0]
    tm = x.shape[0]
    ms = jnp.mean(x * x, axis=-1, keepdims=True)
    n = (x * lax.rsqrt(ms + EPS) * g_ref[...]).astype(BF16)
    pt = lax.dot_general(w_ref[...], n, (((1,), (1,)), ((), ())), preferred_element_type=F32)

    cosa = cosa_ref[...]
    sina = sina_ref[...]
    cosb = cosb_ref[...]
    sinb = sinb_ref[...]
    half = HEAD_DIM // 2
    hrot = PARTIAL_ROT // 2

    def norm_rope_a(t, gain):
        y = t * lax.rsqrt(jnp.mean(t * t, axis=0, keepdims=True) + EPS) * gain
        rot = jnp.concatenate([y[half:], y[:half]], axis=0)
        return y * cosa + rot * sina

    def rope_b(t):
        head = t[:PARTIAL_ROT]
        rot = jnp.concatenate([head[hrot:], head[:hrot]], axis=0)
        return jnp.concatenate([head * cosb + rot * sinb, t[PARTIAL_ROT:]], axis=0)

    qg = qg_ref[...]
    kg = kg_ref[...]
    ones = jnp.ones((ONES_ROWS, tm), BF16)
    for h in range(A_HEADS):
        t = pt[O_AQ + h * HEAD_DIM:O_AQ + (h + 1) * HEAD_DIM]
        qa_ref[0, h * HEAD_DIM:(h + 1) * HEAD_DIM, :] = (norm_rope_a(t, qg) * Q_PRESCALE).astype(BF16)
    for h in range(A_KV_HEADS):
        va_ref[0, h * VA_ROWS:h * VA_ROWS + HEAD_DIM, :] = (
            pt[O_AV + h * HEAD_DIM:O_AV + (h + 1) * HEAD_DIM].astype(BF16))
        va_ref[0, h * VA_ROWS + HEAD_DIM:(h + 1) * VA_ROWS, :] = ones
    for j in range(B_QK_W // HEAD_DIM):
        t = pt[O_BQ + j * HEAD_DIM:O_BQ + (j + 1) * HEAD_DIM]
        qb_ref[0, j * HEAD_DIM:(j + 1) * HEAD_DIM, :] = (rope_b(t) * Q_PRESCALE).astype(BF16)
    for h in range(B_HEADS):
        vb_ref[0, h * VB_ROWS:h * VB_ROWS + B_V_DIM, :] = (
            pt[O_BV + h * B_V_DIM:O_BV + (h + 1) * B_V_DIM].astype(BF16))
        vb_ref[0, h * VB_ROWS + B_V_DIM:(h + 1) * VB_ROWS, :] = ones

    ka = jnp.concatenate([norm_rope_a(pt[O_AK + h * HEAD_DIM:O_AK + (h + 1) * HEAD_DIM], kg)
                          for h in range(A_KV_HEADS)], axis=0)
    ka_ref[0] = ka.T.astype(BF16)
    for h in range(B_HEADS):
        kb = jnp.concatenate([rope_b(pt[O_BK + (2 * h + c) * HEAD_DIM:O_BK + (2 * h + c + 1) * HEAD_DIM])
                              for c in range(2)], axis=0)
        kb_ref[0, h] = kb.T.astype(BF16)


def _in_projection(x, g_attn, w_in_t, tabs, qg, kg):
    bsz, s, d = x.shape
    tm = TOKEN_TILE
    cosa, sina, cosb, sinb = tabs
    const = lambda shape: pl.BlockSpec(shape, lambda b, i: (0,) * len(shape))
    rows = lambda r: pl.BlockSpec((1, r, tm), lambda b, i: (b, 0, i))
    return pl.pallas_call(
        _inproj_kernel,
        grid=(bsz, s // tm),
        in_specs=[
            pl.BlockSpec((1, tm, d), lambda b, i: (b, i, 0)),
            const((1, d)),
            const((IN_W, d)),
            pl.BlockSpec((HEAD_DIM, tm), lambda b, i: (0, i)),
            pl.BlockSpec((HEAD_DIM, tm), lambda b, i: (0, i)),
            pl.BlockSpec((PARTIAL_ROT, tm), lambda b, i: (0, i)),
            pl.BlockSpec((PARTIAL_ROT, tm), lambda b, i: (0, i)),
            const((HEAD_DIM, tm)),
            const((HEAD_DIM, tm)),
        ],
        out_specs=[
            rows(A_Q_W), rows(B_QK_W), rows(A_KV_HEADS * VA_ROWS), rows(B_HEADS * VB_ROWS),
            pl.BlockSpec((1, tm, LANES), lambda b, i: (b, i, 0)),
            pl.BlockSpec((1, B_HEADS, tm, LANES), lambda b, i: (b, 0, i, 0)),
        ],
        out_shape=[
            jax.ShapeDtypeStruct((bsz, A_Q_W, s), BF16),
            jax.ShapeDtypeStruct((bsz, B_QK_W, s), BF16),
            jax.ShapeDtypeStruct((bsz, A_KV_HEADS * VA_ROWS, s), BF16),
            jax.ShapeDtypeStruct((bsz, B_HEADS * VB_ROWS, s), BF16),
            jax.ShapeDtypeStruct((bsz, s, LANES), BF16),
            jax.ShapeDtypeStruct((bsz, B_HEADS, s, LANES), BF16),
        ],
        compiler_params=_cparams(("parallel", "parallel")),
        name="in_projection",
    )(x, g_attn, w_in_t, cosa, sina, cosb, sinb, qg, kg)


def _column_softmax_attention(k_ref, rhs_ref, v_ref, v_rows, acc_ref, s_refs, seq):
    n = rhs_ref.shape[1]
    bk = KV_CHUNK
    gw = COL_GROUP
    n_groups = n // gw
    n_pairs = seq // (2 * bk)

    def scores(c, s_ref, g):
        off = pl.multiple_of(c * bk, bk)
        s_ref[:, g * gw:(g + 1) * gw] = jnp.dot(k_ref[pl.ds(off, bk), :], rhs_ref[:, g * gw:(g + 1) * gw],
                                                preferred_element_type=F32)

    def softmax_pv(c, s_ref, g, m_g):
        off = pl.multiple_of(c * bk, bk)
        m_parts, p_parts = [], []
        for j in range(gw // LANES):
            lo = g * gw + j * LANES
            sj = s_ref[:, lo:lo + LANES]
            mj = jnp.maximum(m_g[:, j * LANES:(j + 1) * LANES], jnp.max(sj, axis=0, keepdims=True))
            p_parts.append(jnp.exp2(sj - mj).astype(BF16))
            m_parts.append(mj)
        m_new = jnp.concatenate(m_parts, axis=1)
        alpha = jnp.exp2(m_g - m_new)
        rl, rh = v_rows(g)
        pv = jnp.dot(v_ref[rl:rh, pl.ds(off, bk)], jnp.concatenate(p_parts, axis=1),
                     preferred_element_type=F32)
        acc_ref[:, g * gw:(g + 1) * gw] = alpha * acc_ref[:, g * gw:(g + 1) * gw] + pv
        return m_new

    def half(c, cur, nxt, ms, issue_next):
        out = []
        for g in range(n_groups):
            if issue_next:
                scores(c + 1, nxt, g)
            out.append(softmax_pv(c, cur, g, ms[g]))
        return tuple(out)

    def pair(j, ms, last):
        c = 2 * j
        ms = half(c, s_refs[0], s_refs[1], ms, True)
        return half(c + 1, s_refs[1], s_refs[0], ms, not last)

    acc_ref[...] = jnp.zeros_like(acc_ref)
    for g in range(n_groups):
        scores(0, s_refs[0], g)
    ms = tuple(jnp.full((1, gw), -jnp.inf, F32) for _ in range(n_groups))
    ms = lax.fori_loop(0, n_pairs - 1, lambda j, ms: pair(j, ms, False), ms)
    pair(n_pairs - 1, ms, True)


def _attn_a_kernel(q_ref, k_ref, v_ref, o_ref, rhs_ref, acc_ref, s0, s1, *, seq):
    bq = A_Q_BLOCK
    half_n = A_GROUP * bq
    rhs_ref[...] = jnp.zeros_like(rhs_ref)
    for h in range(A_HEADS):
        kv = h // A_GROUP
        rhs_ref[kv * HEAD_DIM:(kv + 1) * HEAD_DIM, h * bq:(h + 1) * bq] = q_ref[0, h * HEAD_DIM:(h + 1) * HEAD_DIM, :]
    v_rows = lambda g: ((g * COL_GROUP // half_n) * VA_ROWS, (g * COL_GROUP // half_n + 1) * VA_ROWS)
    _column_softmax_attention(k_ref.at[0], rhs_ref, v_ref.at[0], v_rows, acc_ref, (s0, s1), seq)
    acc = acc_ref[...]
    o = acc[:HEAD_DIM] * (1.0 / acc[HEAD_DIM:HEAD_DIM + 1])
    for h in range(A_HEADS):
        o_ref[0, h * HEAD_DIM:(h + 1) * HEAD_DIM, :] = o[:, h * bq:(h + 1) * bq].astype(BF16)


def _attention_scratch(acc_rows, n):
    return [pltpu.VMEM((LANES, n), BF16), pltpu.VMEM((acc_rows, n), F32),
            pltpu.VMEM((KV_CHUNK, n), F32), pltpu.VMEM((KV_CHUNK, n), F32)]


def _attention_a(qa, ka, va):
    bsz, _, s = qa.shape
    bq = A_Q_BLOCK
    assert s % (2 * KV_CHUNK) == 0 and s % bq == 0
    n = A_HEADS * bq
    return pl.pallas_call(
        functools.partial(_attn_a_kernel, seq=s),
        grid=(bsz, s // bq),
        in_specs=[
            pl.BlockSpec((1, A_Q_W, bq), lambda b, i: (b, 0, i)),
            pl.BlockSpec((1, s, LANES), lambda b, i: (b, 0, 0)),
            pl.BlockSpec((1, A_KV_HEADS * VA_ROWS, s), lambda b, i: (b, 0, 0)),
        ],
        out_specs=pl.BlockSpec((1, A_Q_W, bq), lambda b, i: (b, 0, i)),
        out_shape=jax.ShapeDtypeStruct((bsz, A_Q_W, s), BF16),
        scratch_shapes=_attention_scratch(VA_ROWS, n),
        compiler_params=_cparams(("parallel", "parallel")),
        name="attention_a",
    )(qa, ka, va)


def _attn_b_kernel(lam_ref, q_ref, k_ref, v_ref, sg_ref, o_ref, rhs_ref, acc, s0, s1, *, seq):
    bq = B_Q_BLOCK
    rhs_ref[...] = jnp.zeros_like(rhs_ref)
    rhs_ref[:HEAD_DIM, :bq] = q_ref[0, :HEAD_DIM, :]
    rhs_ref[HEAD_DIM:, bq:] = q_ref[0, HEAD_DIM:, :]
    _column_softmax_attention(k_ref.at[0, 0], rhs_ref, v_ref.at[0], lambda g: (0, VB_ROWS), acc, (s0, s1), seq)
    a = acc[...]
    o = a[:B_V_DIM] * (1.0 / a[B_V_DIM:B_V_DIM + 1])
    o = o[:, :bq] - lam_ref[0] * o[:, bq:]
    o = o * lax.rsqrt(jnp.mean(o * o, axis=0, keepdims=True) + EPS) * sg_ref[...]
    o_ref[0] = (o * (1.0 - LAMBDA_INIT)).astype(BF16)


def _attention_b(lam, qb, kb, vb, subln):
    bsz, _, s = qb.shape
    bq = B_Q_BLOCK
    assert s % (2 * KV_CHUNK) == 0 and s % bq == 0
    n = 2 * bq
    return pl.pallas_call(
        functools.partial(_attn_b_kernel, seq=s),
        grid=(bsz, B_HEADS, s // bq),
        in_specs=[
            pl.BlockSpec(memory_space=pltpu.SMEM),
            pl.BlockSpec((1, B_V_DIM, bq), lambda b, h, i: (b, h, i)),
            pl.BlockSpec((1, 1, s, LANES), lambda b, h, i: (b, h, 0, 0)),
            pl.BlockSpec((1, VB_ROWS, s), lambda b, h, i: (b, h, 0)),
            pl.BlockSpec((B_V_DIM, bq), lambda b, h, i: (0, 0)),
        ],
        out_specs=pl.BlockSpec((1, B_V_DIM, bq), lambda b, h, i: (b, h, i)),
        out_shape=jax.ShapeDtypeStruct((bsz, B_V_W, s), BF16),
        scratch_shapes=_attention_scratch(VB_ROWS, n),
        compiler_params=_cparams(("parallel", "parallel", "parallel")),
        name="attention_b",
    )(lam, qb, kb, vb, subln)


def _outproj_kernel(a_ref, b_ref, x_ref, wa_ref, wb_ref, g_ref, rw_ref, rb_ref,
                    x2_ref, xn_ref, te_ref, tg_ref):
    tn = (((0,), (0,)), ((), ()))
    y = (x_ref[0]
         + lax.dot_general(a_ref[0], wa_ref[...], tn, preferred_element_type=F32)
         + lax.dot_general(b_ref[0], wb_ref[...], tn, preferred_element_type=F32))
    x2_ref[0] = y
    xn = y * lax.rsqrt(jnp.mean(y * y, axis=-1, keepdims=True) + EPS) * g_ref[...]
    xn_ref[0] = xn
    logits = lax.dot_general(rw_ref[...], xn, (((1,), (1,)), ((), ())),
                             precision=lax.Precision.HIGHEST, preferred_element_type=F32) + rb_ref[...]
    iota = lax.broadcasted_iota(jnp.int32, logits.shape, 0)
    work = logits
    vals, idxs = [], []
    for _ in range(TOP_K):
        mx = jnp.max(work, axis=0, keepdims=True)
        idx = jnp.min(jnp.where(work == mx, iota, N_EXPERTS), axis=0, keepdims=True)
        vals.append(mx)
        idxs.append(idx)
        work = jnp.where(iota == idx, -jnp.inf, work)
    ex = [jnp.exp(v - vals[0]) for v in vals]
    inv = 1.0 / (ex[0] + ex[1] + ex[2] + ex[3])
    te_ref[0] = jnp.concatenate(idxs, axis=0)
    tg_ref[0] = jnp.concatenate([e * inv for e in ex], axis=0)


def _out_projection(oa, ob, x, wa, wb, g_ffn, rw_t, rb):
    bsz, s, d = x.shape
    tm = TOKEN_TILE
    const = lambda shape: pl.BlockSpec(shape, lambda b, i: (0,) * len(shape))
    tok = pl.BlockSpec((1, tm, d), lambda b, i: (b, i, 0))
    sel = pl.BlockSpec((1, TOP_K, tm), lambda b, i: (b, 0, i))
    return pl.pallas_call(
        _outproj_kernel,
        grid=(bsz, s // tm),
        in_specs=[
            pl.BlockSpec((1, A_Q_W, tm), lambda b, i: (b, 0, i)),
            pl.BlockSpec((1, B_V_W, tm), lambda b, i: (b, 0, i)),
            tok,
            const((A_Q_W, d)),
            const((B_V_W, d)),
            const((1, d)),
            const((N_EXPERTS, d)),
            const((N_EXPERTS, tm)),
        ],
        out_specs=[tok, tok, sel, sel],
        out_shape=[
            jax.ShapeDtypeStruct((bsz, s, d), F32),
            jax.ShapeDtypeStruct((bsz, s, d), F32),
            jax.ShapeDtypeStruct((bsz, TOP_K, s), jnp.int32),
            jax.ShapeDtypeStruct((bsz, TOP_K, s), F32),
        ],
        compiler_params=_cparams(("parallel", "parallel")),
        name="out_projection_router",
    )(oa, ob, x, wa, wb, g_ffn, rw_t, rb)


def _gather_row(idx_ref, r, src_hbm, buf, sem, slot):
    return pltpu.make_async_copy(src_hbm.at[pl.ds(idx_ref[0, 0, r], 1), :],
                                 buf.at[slot, pl.ds(r, 1), :], sem.at[slot])


def _scatter_row(idx_ref, r, buf, dst_hbm, sem, slot):
    return pltpu.make_async_copy(buf.at[slot, pl.ds(r, 1), :],
                                 dst_hbm.at[pl.ds(idx_ref[0, 0, r], 1), :], sem.at[slot])


def _moe_kernel(be_ref, nb_ref, tok_first, tok_nxt, dst_cur, xn_hbm, wg_ref, bg_ref, wu_ref, bu_ref,
                wd_ref, bd_ref, y_hbm, xbuf, ybuf, gsem, ssem, *, pad_base):
    i = pl.program_id(0)
    n_used = nb_ref[0]
    slot = lax.rem(i, 2)
    rows = MOE_ROWS

    @pl.when(i == 0)
    def _():
        ybuf[0] = jnp.zeros((rows, ybuf.shape[2]), F32)
        fills = [pltpu.make_async_copy(ybuf.at[0], y_hbm.at[pl.ds(pad_base + j * rows, rows), :], ssem.at[0])
                 for j in range(N_EXPERTS)]
        for f in fills:
            f.start()
        for f in fills:
            f.wait()
        for r in range(rows):
            _gather_row(tok_first, r, xn_hbm, xbuf, gsem, 0).start()

    @pl.when(i <= n_used)
    def _():
        pltpu.make_async_copy(xn_hbm.at[pl.ds(0, rows), :], xbuf.at[slot], gsem.at[slot]).wait()

    @pl.when(jnp.logical_and(i >= 2, i - 2 < n_used))
    def _():
        pltpu.make_async_copy(ybuf.at[slot], y_hbm.at[pl.ds(0, rows), :], ssem.at[slot]).wait()

    @pl.when(i < n_used)
    def _():
        xb = xbuf[slot].astype(BF16)
        for r in range(rows):
            _gather_row(tok_nxt, r, xn_hbm, xbuf, gsem, 1 - slot).start()
        g = jnp.dot(xb, wg_ref[0], preferred_element_type=F32) + bg_ref[0]
        u = jnp.dot(xb, wu_ref[0], preferred_element_type=F32) + bu_ref[0]
        g = jnp.minimum(g, SWIGLU_LIMIT)
        u = jnp.clip(u, -SWIGLU_LIMIT, SWIGLU_LIMIT)
        h = g * jax.nn.sigmoid(g * SWIGLU_ALPHA) * (u + 1.0)
        ybuf[slot] = jnp.dot(h.astype(BF16), wd_ref[0], preferred_element_type=F32) + bd_ref[0]
        for r in range(rows):
            _scatter_row(dst_cur, r, ybuf, y_hbm, ssem, slot).start()


def _moe_experts(block_e, n_used, src_tok, dst_row, n_out_rows, xn, wg, bg, wu, bu, wd, bd):
    t, d = xn.shape
    bm = MOE_ROWS
    n_blocks = src_tok.shape[0]
    dff = wg.shape[2]
    last = n_blocks - 1
    n_steps = n_blocks + 2
    blk = lambda i: jnp.minimum(i, last)
    wspec = lambda k, n: pl.BlockSpec((1, k, n), lambda i, be, nb: (be[blk(i)], 0, 0))
    idx = lambda fn: pl.BlockSpec((1, 1, bm), lambda i, be, nb: (fn(i), 0, 0), memory_space=pltpu.SMEM)
    grid_spec = pltpu.PrefetchScalarGridSpec(
        num_scalar_prefetch=2,
        grid=(n_steps,),
        in_specs=[
            idx(lambda i: 0), idx(lambda i: blk(i + 1)), idx(blk),
            pl.BlockSpec(memory_space=pl.ANY),
            wspec(d, dff), wspec(1, dff), wspec(d, dff), wspec(1, dff), wspec(dff, d), wspec(1, d),
        ],
        out_specs=pl.BlockSpec(memory_space=pl.ANY),
        scratch_shapes=[pltpu.VMEM((2, bm, d), F32), pltpu.VMEM((2, bm, d), F32),
                        pltpu.SemaphoreType.DMA((2,)), pltpu.SemaphoreType.DMA((2,))],
    )
    return pl.pallas_call(
        functools.partial(_moe_kernel, pad_base=t * TOP_K),
        grid_spec=grid_spec,
        out_shape=jax.ShapeDtypeStruct((n_out_rows, d), F32),
        compiler_params=_cparams(("arbitrary",)),
        name="moe_experts",
    )(block_e, n_used, src_tok, src_tok, dst_row, xn, wg, bg, wu, bu, wd, bd)


def _combine_kernel(y_ref, x2_ref, gate_ref, g_ref, o_ref):
    d = x2_ref.shape[1]
    gates = gate_ref[...]
    y = x2_ref[...]
    for k in range(TOP_K):
        y = y + gates[:, k:k + 1] * y_ref[:, k * d:(k + 1) * d]
    o_ref[...] = y * lax.rsqrt(jnp.mean(y * y, axis=-1, keepdims=True) + EPS) * g_ref[...]


def _combine(y4, x2, gates, g_final):
    t, d = x2.shape
    tc = COMBINE_TOKENS
    return pl.pallas_call(
        _combine_kernel,
        grid=(t // tc,),
        in_specs=[
            pl.BlockSpec((tc, TOP_K * d), lambda i: (i, 0)),
            pl.BlockSpec((tc, d), lambda i: (i, 0)),
            pl.BlockSpec((tc, TOP_K), lambda i: (i, 0)),
            pl.BlockSpec((1, d), lambda i: (0, 0)),
        ],
        out_specs=pl.BlockSpec((tc, d), lambda i: (i, 0)),
        out_shape=jax.ShapeDtypeStruct((t, d), F32),
        compiler_params=_cparams(("parallel",)),
        name="combine_final_norm",
    )(y4, x2, gates, g_final)


def _route(top_e):
    t = top_e.shape[0]
    a = t * TOP_K
    bm = MOE_ROWS
    n_blocks = a // bm + N_EXPERTS
    flat_e = top_e.reshape(-1)
    order = jnp.argsort(flat_e, stable=True).astype(jnp.int32)
    counts = jnp.bincount(flat_e, length=N_EXPERTS).astype(jnp.int32)
    padded = ((counts + bm - 1) // bm) * bm
    start = jnp.cumsum(counts) - counts
    pend = jnp.cumsum(padded)
    pstart = pend - padded
    block_e = jnp.minimum(jnp.searchsorted(pend, jnp.arange(n_blocks, dtype=jnp.int32) * bm, side='right'),
                          N_EXPERTS - 1).astype(jnp.int32)
    rows = jnp.arange(n_blocks * bm, dtype=jnp.int32)
    row_e = jnp.repeat(block_e, bm)
    within = rows - pstart[row_e]
    valid = within < counts[row_e]
    assign = order[jnp.clip(start[row_e] + within, 0, a - 1)]
    pad_rank = jnp.cumsum(jnp.logical_not(valid).astype(jnp.int32)) - 1
    src = jnp.where(valid, assign // TOP_K, 0).astype(jnp.int32)
    dst = jnp.where(valid, assign, a + pad_rank).astype(jnp.int32)
    n_used = (pend[-1] // bm).astype(jnp.int32).reshape(1)
    return block_e, n_used, src.reshape(n_blocks, 1, bm), dst.reshape(n_blocks, 1, bm), a + N_EXPERTS * bm


def _rope_tables(seq):
    rows = seq // GRID_W
    row = jnp.repeat(jnp.arange(rows, dtype=F32), GRID_W)
    col = jnp.tile(jnp.arange(GRID_W, dtype=F32), rows)
    half = HEAD_DIM // 2
    inv = AXIAL_THETA ** (-jnp.arange(0, half, 2, dtype=F32) / half)
    ang = jnp.concatenate([row[:, None] * inv, col[:, None] * inv], axis=-1)
    ang = jnp.concatenate([ang, ang], axis=-1).T
    sign_a = jnp.where(jnp.arange(HEAD_DIM) < half, -1.0, 1.0).astype(F32)[:, None]
    tt = jnp.arange(seq, dtype=F32)
    invb = ROPE_THETA ** (-jnp.arange(0, PARTIAL_ROT, 2, dtype=F32) / PARTIAL_ROT)
    angb = tt[:, None] * invb
    angb = jnp.concatenate([angb, angb], axis=-1).T
    sign_b = jnp.where(jnp.arange(PARTIAL_ROT) < PARTIAL_ROT // 2, -1.0, 1.0).astype(F32)[:, None]
    return jnp.cos(ang), jnp.sin(ang) * sign_a, jnp.cos(angb), jnp.sin(angb) * sign_b


def _trunk(x, p):
    bsz, s, d = x.shape
    t = bsz * s
    qa, qb, va, vb, ka, kb = _in_projection(x, p["g_attn"], p["w_in_t"], _rope_tables(s), p["qg"], p["kg"])
    oa = _attention_a(qa, ka, va)
    ob = _attention_b(p["lam"], qb, kb, vb, p["subln"])
    x2, xn, te, tg = _out_projection(oa, ob, x, p["wa"], p["wb"], p["g_ffn"], p["rw_t"], p["rb"])
    top_e = jnp.swapaxes(te, 1, 2).reshape(t, TOP_K)
    gates = jnp.swapaxes(tg, 1, 2).reshape(t, TOP_K)
    block_e, n_used, src_tok, dst_row, n_out_rows = _route(top_e)
    y = _moe_experts(block_e, n_used, src_tok, dst_row, n_out_rows, xn.reshape(t, d),
                     p["wg"], p["bg"], p["wu"], p["bu"], p["wd"], p["bd"])
    y4 = y.reshape(n_out_rows // TOP_K, TOP_K * d)
    out = _combine(y4, x2.reshape(t, d), gates, p["g_final"])
    return out.reshape(bsz, s, d)


def kernel(x_prompt, x_sample, attn_norm, w_in, a_q_norm, a_k_norm, b_lambda_q1, b_lambda_k1, b_lambda_q2,
           b_lambda_k2, b_subln, w_out, ffn_norm, router_w, router_b, w_gate, b_gate, w_up, b_up, w_down,
           b_down, final_norm):
    tm = TOKEN_TILE
    lam = (jnp.exp(jnp.sum(b_lambda_q1[0].astype(F32) * b_lambda_k1[0].astype(F32)))
           - jnp.exp(jnp.sum(b_lambda_q2[0].astype(F32) * b_lambda_k2[0].astype(F32)))
           + LAMBDA_INIT)
    p = {
        "g_attn": attn_norm[0].reshape(1, D_MODEL),
        "w_in_t": w_in[0].T.astype(BF16),
        "qg": jnp.broadcast_to(a_q_norm[0][:, None], (HEAD_DIM, tm)),
        "kg": jnp.broadcast_to(a_k_norm[0][:, None], (HEAD_DIM, tm)),
        "lam": lam.reshape(1).astype(F32),
        "subln": jnp.broadcast_to(b_subln[0][:, None], (B_V_DIM, B_Q_BLOCK)),
        "wa": w_out[0, :A_Q_W].astype(BF16),
        "wb": w_out[0, A_Q_W:].astype(BF16),
        "g_ffn": ffn_norm[0].reshape(1, D_MODEL),
        "rw_t": router_w[0].T,
        "rb": jnp.broadcast_to(router_b[0][:, None], (N_EXPERTS, tm)),
        "wg": w_gate[0].astype(BF16),
        "bg": b_gate[0].reshape(N_EXPERTS, 1, -1),
        "wu": w_up[0].astype(BF16),
        "bu": b_up[0].reshape(N_EXPERTS, 1, -1),
        "wd": w_down[0].astype(BF16),
        "bd": b_down[0].reshape(N_EXPERTS, 1, -1),
        "g_final": final_norm.reshape(1, D_MODEL),
    }
    return _trunk(x_prompt, p), _trunk(x_sample, p)
```

```python
import functools
import math

import jax
import jax.numpy as jnp
from jax import lax
from jax.experimental import pallas as pl
from jax.experimental.pallas import tpu as pltpu

F32 = jnp.float32
BF16 = jnp.bfloat16

D_MODEL = 1024
HEAD_DIM = 64
A_HEADS = 8
A_KV_HEADS = 2
A_GROUP = A_HEADS // A_KV_HEADS
B_HEADS = 4
B_V_DIM = 2 * HEAD_DIM
A_Q_W = A_HEADS * HEAD_DIM
A_KV_W = A_KV_HEADS * HEAD_DIM
B_QK_W = B_HEADS * 2 * HEAD_DIM
B_V_W = B_HEADS * B_V_DIM
IN_W = A_Q_W + 2 * A_KV_W + 2 * B_QK_W + B_V_W
GRID_W = 64
AXIAL_THETA = 10000.0
ROPE_THETA = 500000.0
PARTIAL_ROT = HEAD_DIM // 4
N_EXPERTS = 32
TOP_K = 4
SWIGLU_LIMIT = 7.0
SWIGLU_ALPHA = 1.702
EPS = 1e-6
LAMBDA_INIT = 0.8 - 0.6 * math.exp(-0.3 * 0)
Q_PRESCALE = HEAD_DIM ** -0.5 * math.log2(math.e)

O_AQ = 0
O_AK = O_AQ + A_Q_W
O_AV = O_AK + A_KV_W
O_BQ = O_AV + A_KV_W
O_BK = O_BQ + B_QK_W
O_BV = O_BK + B_QK_W

LANES = 128
TILE_SUBLANES = 8
BF16_SUBLANES = 16
ONES_ROWS = BF16_SUBLANES
VA_ROWS = HEAD_DIM + ONES_ROWS
VB_ROWS = B_V_DIM + ONES_ROWS
TOKEN_TILE = 512
A_Q_BLOCK = 128
B_Q_BLOCK = 512
KV_CHUNK = 256
COL_GROUP = 256
MOE_ROWS = 256
COMBINE_TOKENS = 256
VMEM_LIMIT = 48 * 1024 * 1024


def _cparams(sem):
    return pltpu.CompilerParams(dimension_semantics=sem, vmem_limit_bytes=VMEM_LIMIT)


def _inproj_kernel(x_ref, g_ref, w_ref, cosa_ref, sina_ref, cosb_ref, sinb_ref, qg_ref, kg_ref,
                   qa_ref, qb_ref, va_ref, vb_ref, ka_ref, kb_ref):
    x = x_ref[0]
    tm = x.shape[0]
    ms = jnp.mean(x * x, axis=-1, keepdims=True)
    n = (x * lax.rsqrt(ms + EPS) * g_ref[...]).astype(BF16)
    pt = lax.dot_general(w_ref[...], n, (((1,), (1,)), ((), ())), preferred_element_type=F32)

    cosa = cosa_ref[...]
    sina = sina_ref[...]
    cosb = cosb_ref[...]
    sinb = sinb_ref[...]
    half = HEAD_DIM // 2
    hrot = PARTIAL_ROT // 2

    def norm_rope_a(t, gain):
        y = t * lax.rsqrt(jnp.mean(t * t, axis=0, keepdims=True) + EPS) * gain
        rot = jnp.concatenate([y[half:], y[:half]], axis=0)
        return y * cosa + rot * sina

    def rope_b(t):
        head = t[:PARTIAL_ROT]
        rot = jnp.concatenate([head[hrot:], head[:hrot]], axis=0)
        return jnp.concatenate([head * cosb + rot * sinb, t[PARTIAL_ROT:]], axis=0)

    qg = qg_ref[...]
    kg = kg_ref[...]
    ones = jnp.ones((ONES_ROWS, tm), BF16)
    for h in range(A_HEADS):
        t = pt[O_AQ + h * HEAD_DIM:O_AQ + (h + 1) * HEAD_DIM]
        qa_ref[0, h * HEAD_DIM:(h + 1) * HEAD_DIM, :] = (norm_rope_a(t, qg) * Q_PRESCALE).astype(BF16)
    for h in range(A_KV_HEADS):
        va_ref[0, h * VA_ROWS:h * VA_ROWS + HEAD_DIM, :] = (
            pt[O_AV + h * HEAD_DIM:O_AV + (h + 1) * HEAD_DIM].astype(BF16))
        va_ref[0, h * VA_ROWS + HEAD_DIM:(h + 1) * VA_ROWS, :] = ones
    for j in range(B_QK_W // HEAD_DIM):
        t = pt[O_BQ + j * HEAD_DIM:O_BQ + (j + 1) * HEAD_DIM]
        qb_ref[0, j * HEAD_DIM:(j + 1) * HEAD_DIM, :] = (rope_b(t) * Q_PRESCALE).astype(BF16)
    for h in range(B_HEADS):
        vb_ref[0, h * VB_ROWS:h * VB_ROWS + B_V_DIM, :] = (
            pt[O_BV + h * B_V_DIM:O_BV + (h + 1) * B_V_DIM].astype(BF16))
        vb_ref[0, h * VB_ROWS + B_V_DIM:(h + 1) * VB_ROWS, :] = ones

    ka = jnp.concatenate([norm_rope_a(pt[O_AK + h * HEAD_DIM:O_AK + (h + 1) * HEAD_DIM], kg)
                          for h in range(A_KV_HEADS)], axis=0)
    ka_ref[0] = ka.T.astype(BF16)
    for h in range(B_HEADS):
        kb = jnp.concatenate([rope_b(pt[O_BK + (2 * h + c) * HEAD_DIM:O_BK + (2 * h + c + 1) * HEAD_DIM])
                              for c in range(2)], axis=0)
        kb_ref[0, h] = kb.T.astype(BF16)


def _in_projection(x, g_attn, w_in_t, tabs, qg, kg):
    bsz, s, d = x.shape
    tm = TOKEN_TILE
    cosa, sina, cosb, sinb = tabs
    const = lambda shape: pl.BlockSpec(shape, lambda b, i: (0,) * len(shape))
    rows = lambda r: pl.BlockSpec((1, r, tm), lambda b, i: (b, 0, i))
    return pl.pallas_call(
        _inproj_kernel,
        grid=(bsz, s // tm),
        in_specs=[
            pl.BlockSpec((1, tm, d), lambda b, i: (b, i, 0)),
            const((1, d)),
            const((IN_W, d)),
            pl.BlockSpec((HEAD_DIM, tm), lambda b, i: (0, i)),
            pl.BlockSpec((HEAD_DIM, tm), lambda b, i: (0, i)),
            pl.BlockSpec((PARTIAL_ROT, tm), lambda b, i: (0, i)),
            pl.BlockSpec((PARTIAL_ROT, tm), lambda b, i: (0, i)),
            const((HEAD_DIM, tm)),
            const((HEAD_DIM, tm)),
        ],
        out_specs=[
            rows(A_Q_W), rows(B_QK_W), rows(A_KV_HEADS * VA_ROWS), rows(B_HEADS * VB_ROWS),
            pl.BlockSpec((1, tm, LANES), lambda b, i: (b, i, 0)),
            pl.BlockSpec((1, B_HEADS, tm, LANES), lambda b, i: (b, 0, i, 0)),
        ],
        out_shape=[
            jax.ShapeDtypeStruct((bsz, A_Q_W, s), BF16),
            jax.ShapeDtypeStruct((bsz, B_QK_W, s), BF16),
            jax.ShapeDtypeStruct((bsz, A_KV_HEADS * VA_ROWS, s), BF16),
            jax.ShapeDtypeStruct((bsz, B_HEADS * VB_ROWS, s), BF16),
            jax.ShapeDtypeStruct((bsz, s, LANES), BF16),
            jax.ShapeDtypeStruct((bsz, B_HEADS, s, LANES), BF16),
        ],
        compiler_params=_cparams(("parallel", "parallel")),
        name="in_projection",
    )(x, g_attn, w_in_t, cosa, sina, cosb, sinb, qg, kg)


def _column_softmax_attention(k_ref, rhs_ref, v_ref, v_rows, acc_ref, s_refs, seq):
    n = rhs_ref.shape[1]
    bk = KV_CHUNK
    gw = COL_GROUP
    n_groups = n // gw
    n_pairs = seq // (2 * bk)

    def scores(c, s_ref, g):
        off = pl.multiple_of(c * bk, bk)
        s_ref[:, g * gw:(g + 1) * gw] = jnp.dot(k_ref[pl.ds(off, bk), :], rhs_ref[:, g * gw:(g + 1) * gw],
                                                preferred_element_type=F32)

    def softmax_pv(c, s_ref, g, m_g):
        off = pl.multiple_of(c * bk, bk)
        m_parts, p_parts = [], []
        for j in range(gw // LANES):
            lo = g * gw + j * LANES
            sj = s_ref[:, lo:lo + LANES]
            mj = jnp.maximum(m_g[:, j * LANES:(j + 1) * LANES], jnp.max(sj, axis=0, keepdims=True))
            p_parts.append(jnp.exp2(sj - mj).astype(BF16))
            m_parts.append(mj)
        m_new = jnp.concatenate(m_parts, axis=1)
        alpha = jnp.exp2(m_g - m_new)
        rl, rh = v_rows(g)
        pv = jnp.dot(v_ref[rl:rh, pl.ds(off, bk)], jnp.concatenate(p_parts, axis=1),
                     preferred_element_type=F32)
        acc_ref[:, g * gw:(g + 1) * gw] = alpha * acc_ref[:, g * gw:(g + 1) * gw] + pv
        return m_new

    def half(c, cur, nxt, ms, issue_next):
        out = []
        for g in range(n_groups):
            if issue_next:
                scores(c + 1, nxt, g)
            out.append(softmax_pv(c, cur, g, ms[g]))
        return tuple(out)

    def pair(j, ms, last):
        c = 2 * j
        ms = half(c, s_refs[0], s_refs[1], ms, True)
        return half(c + 1, s_refs[1], s_refs[0], ms, not last)

    acc_ref[...] = jnp.zeros_like(acc_ref)
    for g in range(n_groups):
        scores(0, s_refs[0], g)
    ms = tuple(jnp.full((1, gw), -jnp.inf, F32) for _ in range(n_groups))
    ms = lax.fori_loop(0, n_pairs - 1, lambda j, ms: pair(j, ms, False), ms)
    pair(n_pairs - 1, ms, True)


def _attn_a_kernel(q_ref, k_ref, v_ref, o_ref, rhs_ref, acc_ref, s0, s1, *, seq):
    bq = A_Q_BLOCK
    half_n = A_GROUP * bq
    rhs_ref[...] = jnp.zeros_like(rhs_ref)
    for h in range(A_HEADS):
        kv = h // A_GROUP
        rhs_ref[kv * HEAD_DIM:(kv + 1) * HEAD_DIM, h * bq:(h + 1) * bq] = q_ref[0, h * HEAD_DIM:(h + 1) * HEAD_DIM, :]
    v_rows = lambda g: ((g * COL_GROUP // half_n) * VA_ROWS, (g * COL_GROUP // half_n + 1) * VA_ROWS)
    _column_softmax_attention(k_ref.at[0], rhs_ref, v_ref.at[0], v_rows, acc_ref, (s0, s1), seq)
    acc = acc_ref[...]
    o = acc[:HEAD_DIM] * (1.0 / acc[HEAD_DIM:HEAD_DIM + 1])
    for h in range(A_HEADS):
        o_ref[0, h * HEAD_DIM:(h + 1) * HEAD_DIM, :] = o[:, h * bq:(h + 1) * bq].astype(BF16)


def _attention_scratch(acc_rows, n):
    return [pltpu.VMEM((LANES, n), BF16), pltpu.VMEM((acc_rows, n), F32),
            pltpu.VMEM((KV_CHUNK, n), F32), pltpu.VMEM((KV_CHUNK, n), F32)]


def _attention_a(qa, ka, va):
    bsz, _, s = qa.shape
    bq = A_Q_BLOCK
    assert s % (2 * KV_CHUNK) == 0 and s % bq == 0
    n = A_HEADS * bq
    return pl.pallas_call(
        functools.partial(_attn_a_kernel, seq=s),
        grid=(bsz, s // bq),
        in_specs=[
            pl.BlockSpec((1, A_Q_W, bq), lambda b, i: (b, 0, i)),
            pl.BlockSpec((1, s, LANES), lambda b, i: (b, 0, 0)),
            pl.BlockSpec((1, A_KV_HEADS * VA_ROWS, s), lambda b, i: (b, 0, 0)),
        ],
        out_specs=pl.BlockSpec((1, A_Q_W, bq), lambda b, i: (b, 0, i)),
        out_shape=jax.ShapeDtypeStruct((bsz, A_Q_W, s), BF16),
        scratch_shapes=_attention_scratch(VA_ROWS, n),
        compiler_params=_cparams(("parallel", "parallel")),
        name="attention_a",
    )(qa, ka, va)


def _attn_b_kernel(lam_ref, q_ref, k_ref, v_ref, sg_ref, o_ref, rhs_ref, acc, s0, s1, *, seq):
    bq = B_Q_BLOCK
    rhs_ref[...] = jnp.zeros_like(rhs_ref)
    rhs_ref[:HEAD_DIM, :bq] = q_ref[0, :HEAD_DIM, :]
    rhs_ref[HEAD_DIM:, bq:] = q_ref[0, HEAD_DIM:, :]
    _column_softmax_attention(k_ref.at[0, 0], rhs_ref, v_ref.at[0], lambda g: (0, VB_ROWS), acc, (s0, s1), seq)
    a = acc[...]
    o = a[:B_V_DIM] * (1.0 / a[B_V_DIM:B_V_DIM + 1])
    o = o[:, :bq] - lam_ref[0] * o[:, bq:]
    o = o * lax.rsqrt(jnp.mean(o * o, axis=0, keepdims=True) + EPS) * sg_ref[...]
    o_ref[0] = (o * (1.0 - LAMBDA_INIT)).astype(BF16)


def _attention_b(lam, qb, kb, vb, subln):
    bsz, _, s = qb.shape
    bq = B_Q_BLOCK
    assert s % (2 * KV_CHUNK) == 0 and s % bq == 0
    n = 2 * bq
    return pl.pallas_call(
        functools.partial(_attn_b_kernel, seq=s),
        grid=(bsz, B_HEADS, s // bq),
        in_specs=[
            pl.BlockSpec(memory_space=pltpu.SMEM),
            pl.BlockSpec((1, B_V_DIM, bq), lambda b, h, i: (b, h, i)),
            pl.BlockSpec((1, 1, s, LANES), lambda b, h, i: (b, h, 0, 0)),
            pl.BlockSpec((1, VB_ROWS, s), lambda b, h, i: (b, h, 0)),
            pl.BlockSpec((B_V_DIM, bq), lambda b, h, i: (0, 0)),
        ],
        out_specs=pl.BlockSpec((1, B_V_DIM, bq), lambda b, h, i: (b, h, i)),
        out_shape=jax.ShapeDtypeStruct((bsz, B_V_W, s), BF16),
        scratch_shapes=_attention_scratch(VB_ROWS, n),
        compiler_params=_cparams(("parallel", "parallel", "parallel")),
        name="attention_b",
    )(lam, qb, kb, vb, subln)


def _store_token_tiles(ref, val):
    n = val.shape[0]
    for j in range(TILE_SUBLANES):
        ref[pl.ds(j, n, stride=TILE_SUBLANES), :] = val[:, j * LANES:(j + 1) * LANES]


def _load_token_tiles(ref, n):
    return jnp.concatenate([ref[pl.ds(j, n, stride=TILE_SUBLANES), :] for j in range(TILE_SUBLANES)], axis=1)


def _outproj_kernel(a_ref, b_ref, x_ref, wa_ref, wb_ref, g_ref, rw_ref, rb_ref,
                    x2_ref, xn_ref, te_ref, tg_ref):
    tn = (((0,), (0,)), ((), ()))
    y = (x_ref[0]
         + lax.dot_general(a_ref[0], wa_ref[...], tn, preferred_element_type=F32)
         + lax.dot_general(b_ref[0], wb_ref[...], tn, preferred_element_type=F32))
    x2_ref[0] = y
    xn = y * lax.rsqrt(jnp.mean(y * y, axis=-1, keepdims=True) + EPS) * g_ref[...]
    _store_token_tiles(xn_ref, xn)
    logits = lax.dot_general(rw_ref[...], xn, (((1,), (1,)), ((), ())),
                             precision=lax.Precision.HIGHEST, preferred_element_type=F32) + rb_ref[...]
    iota = lax.broadcasted_iota(jnp.int32, logits.shape, 0)
    work = logits
    vals, idxs = [], []
    for _ in range(TOP_K):
        mx = jnp.max(work, axis=0, keepdims=True)
        idx = jnp.min(jnp.where(work == mx, iota, N_EXPERTS), axis=0, keepdims=True)
        vals.append(mx)
        idxs.append(idx)
        work = jnp.where(iota == idx, -jnp.inf, work)
    ex = [jnp.exp(v - vals[0]) for v in vals]
    inv = 1.0 / (ex[0] + ex[1] + ex[2] + ex[3])
    te_ref[0] = jnp.concatenate(idxs, axis=0)
    tg_ref[0] = jnp.concatenate([e * inv for e in ex], axis=0)


def _out_projection(oa, ob, x, wa, wb, g_ffn, rw_t, rb):
    bsz, s, d = x.shape
    tm = TOKEN_TILE
    const = lambda shape: pl.BlockSpec(shape, lambda b, i: (0,) * len(shape))
    tok = pl.BlockSpec((1, tm, d), lambda b, i: (b, i, 0))
    sel = pl.BlockSpec((1, TOP_K, tm), lambda b, i: (b, 0, i))
    return pl.pallas_call(
        _outproj_kernel,
        grid=(bsz, s // tm),
        in_specs=[
            pl.BlockSpec((1, A_Q_W, tm), lambda b, i: (b, 0, i)),
            pl.BlockSpec((1, B_V_W, tm), lambda b, i: (b, 0, i)),
            tok,
            const((A_Q_W, d)),
            const((B_V_W, d)),
            const((1, d)),
            const((N_EXPERTS, d)),
            const((N_EXPERTS, tm)),
        ],
        out_specs=[tok,
                   pl.BlockSpec((tm * TILE_SUBLANES, LANES), lambda b, i: (b * (s // tm) + i, 0)),
                   sel, sel],
        out_shape=[
            jax.ShapeDtypeStruct((bsz, s, d), F32),
            jax.ShapeDtypeStruct((bsz * s * TILE_SUBLANES, LANES), F32),
            jax.ShapeDtypeStruct((bsz, TOP_K, s), jnp.int32),
            jax.ShapeDtypeStruct((bsz, TOP_K, s), F32),
        ],
        compiler_params=_cparams(("parallel", "parallel")),
        name="out_projection_router",
    )(oa, ob, x, wa, wb, g_ffn, rw_t, rb)


def _tile_rows(r):
    return pl.ds(r * TILE_SUBLANES, TILE_SUBLANES)


def _gather_row(idx_ref, r, src_hbm, buf, sem):
    src = pl.ds(pl.multiple_of(idx_ref[0, 0, r], TILE_SUBLANES), TILE_SUBLANES)
    return pltpu.make_async_copy(src_hbm.at[src, :], buf.at[_tile_rows(r), :], sem)


def _scatter_row(idx_ref, r, buf, dst_hbm, sem):
    dst = pl.ds(pl.multiple_of(idx_ref[0, 0, r], TILE_SUBLANES), TILE_SUBLANES)
    return pltpu.make_async_copy(buf.at[_tile_rows(r), :], dst_hbm.at[dst, :], sem)


def _moe_kernel(be_ref, nb_ref, tok_first, tok_nxt, dst_cur, xn_hbm, wg_ref, bg_ref, wu_ref, bu_ref,
                wd_ref, bd_ref, y_hbm, xbuf0, xbuf1, ybuf0, ybuf1, gsem, ssem, *, pad_base):
    i = pl.program_id(0)
    n_used = nb_ref[0]
    rows = MOE_ROWS
    tile_rows = rows * TILE_SUBLANES
    xbufs = (xbuf0, xbuf1)
    ybufs = (ybuf0, ybuf1)

    @pl.when(i == 0)
    def _():
        ybuf0[...] = jnp.zeros_like(ybuf0)
        fills = [pltpu.make_async_copy(ybuf0, y_hbm.at[pl.ds(pad_base + j * tile_rows, tile_rows), :],
                                       ssem.at[0]) for j in range(N_EXPERTS)]
        for f in fills:
            f.start()
        for f in fills:
            f.wait()
        for r in range(rows):
            _gather_row(tok_first, r, xn_hbm, xbuf0, gsem.at[0]).start()

    def step(slot):
        xbuf, ybuf = xbufs[slot], ybufs[slot]

        @pl.when(i <= n_used)
        def _():
            pltpu.make_async_copy(xn_hbm.at[pl.ds(0, tile_rows), :], xbuf, gsem.at[slot]).wait()

        @pl.when(jnp.logical_and(i >= 2, i - 2 < n_used))
        def _():
            pltpu.make_async_copy(ybuf, y_hbm.at[pl.ds(0, tile_rows), :], ssem.at[slot]).wait()

        @pl.when(i < n_used)
        def _():
            xb = _load_token_tiles(xbuf, rows).astype(BF16)
            for r in range(rows):
                _gather_row(tok_nxt, r, xn_hbm, xbufs[1 - slot], gsem.at[1 - slot]).start()
            g = jnp.dot(xb, wg_ref[0], preferred_element_type=F32) + bg_ref[0]
            u = jnp.dot(xb, wu_ref[0], preferred_element_type=F32) + bu_ref[0]
            g = jnp.minimum(g, SWIGLU_LIMIT)
            u = jnp.clip(u, -SWIGLU_LIMIT, SWIGLU_LIMIT)
            h = g * jax.nn.sigmoid(g * SWIGLU_ALPHA) * (u + 1.0)
            y = jnp.dot(h.astype(BF16), wd_ref[0], preferred_element_type=F32) + bd_ref[0]
            _store_token_tiles(ybuf, y)
            for r in range(rows):
                _scatter_row(dst_cur, r, ybuf, y_hbm, ssem.at[slot]).start()

    for slot in range(2):
        pl.when(lax.rem(i, 2) == slot)(functools.partial(step, slot))


def _moe_experts(block_e, n_used, src_tok, dst_row, n_out_rows, xn, wg, bg, wu, bu, wd, bd):
    d = D_MODEL
    t = xn.shape[0] // TILE_SUBLANES
    bm = MOE_ROWS
    n_blocks = src_tok.shape[0]
    dff = wg.shape[2]
    last = n_blocks - 1
    n_steps = n_blocks + 2
    blk = lambda i: jnp.minimum(i, last)
    wspec = lambda k, n: pl.BlockSpec((1, k, n), lambda i, be, nb: (be[blk(i)], 0, 0))
    idx = lambda fn: pl.BlockSpec((1, 1, bm), lambda i, be, nb: (fn(i), 0, 0), memory_space=pltpu.SMEM)
    grid_spec = pltpu.PrefetchScalarGridSpec(
        num_scalar_prefetch=2,
        grid=(n_steps,),
        in_specs=[
            idx(lambda i: 0), idx(lambda i: blk(i + 1)), idx(blk),
            pl.BlockSpec(memory_space=pl.ANY),
            wspec(d, dff), wspec(1, dff), wspec(d, dff), wspec(1, dff), wspec(dff, d), wspec(1, d),
        ],
        out_specs=pl.BlockSpec(memory_space=pl.ANY),
        scratch_shapes=[pltpu.VMEM((bm * TILE_SUBLANES, LANES), F32)] * 4
                       + [pltpu.SemaphoreType.DMA((2,)), pltpu.SemaphoreType.DMA((2,))],
    )
    return pl.pallas_call(
        functools.partial(_moe_kernel, pad_base=t * TOP_K * TILE_SUBLANES),
        grid_spec=grid_spec,
        out_shape=jax.ShapeDtypeStruct((n_out_rows * TILE_SUBLANES, LANES), F32),
        compiler_params=_cparams(("arbitrary",)),
        name="moe_experts",
    )(block_e, n_used, src_tok, src_tok, dst_row, xn, wg, bg, wu, bu, wd, bd)


def _combine_kernel(y0_ref, y1_ref, y2_ref, y3_ref, x2_ref, gate_ref, g_ref, o_ref):
    tc = x2_ref.shape[0]
    gates = gate_ref[...]
    y = x2_ref[...]
    for k, y_ref in enumerate((y0_ref, y1_ref, y2_ref, y3_ref)):
        y = y + gates[:, k:k + 1] * _load_token_tiles(y_ref, tc)
    o_ref[...] = y * lax.rsqrt(jnp.mean(y * y, axis=-1, keepdims=True) + EPS) * g_ref[...]


def _combine(y, x2, gates, g_final):
    t, d = x2.shape
    tc = COMBINE_TOKENS
    n = t // tc
    yspec = lambda k: pl.BlockSpec((tc * TILE_SUBLANES, LANES), lambda i: (k * n + i, 0))
    return pl.pallas_call(
        _combine_kernel,
        grid=(n,),
        in_specs=[
            yspec(0), yspec(1), yspec(2), yspec(3),
            pl.BlockSpec((tc, d), lambda i: (i, 0)),
            pl.BlockSpec((tc, TOP_K), lambda i: (i, 0)),
            pl.BlockSpec((1, d), lambda i: (0, 0)),
        ],
        out_specs=pl.BlockSpec((tc, d), lambda i: (i, 0)),
        out_shape=jax.ShapeDtypeStruct((t, d), F32),
        compiler_params=_cparams(("parallel",)),
        name="combine_final_norm",
    )(y, y, y, y, x2, gates, g_final)


def _route(top_e):
    t = top_e.shape[0]
    a = t * TOP_K
    bm = MOE_ROWS
    n_blocks = a // bm + N_EXPERTS
    flat_e = top_e.reshape(-1)
    order = jnp.argsort(flat_e, stable=True).astype(jnp.int32)
    counts = jnp.bincount(flat_e, length=N_EXPERTS).astype(jnp.int32)
    padded = ((counts + bm - 1) // bm) * bm
    start = jnp.cumsum(counts) - counts
    pend = jnp.cumsum(padded)
    pstart = pend - padded
    block_e = jnp.minimum(jnp.searchsorted(pend, jnp.arange(n_blocks, dtype=jnp.int32) * bm, side='right'),
                          N_EXPERTS - 1).astype(jnp.int32)
    rows = jnp.arange(n_blocks * bm, dtype=jnp.int32)
    row_e = jnp.repeat(block_e, bm)
    within = rows - pstart[row_e]
    valid = within < counts[row_e]
    assign = order[jnp.clip(start[row_e] + within, 0, a - 1)]
    pad_rank = jnp.cumsum(jnp.logical_not(valid).astype(jnp.int32)) - 1
    tok = assign // TOP_K
    src = (jnp.where(valid, tok, 0) * TILE_SUBLANES).astype(jnp.int32)
    dst = (jnp.where(valid, (assign % TOP_K) * t + tok, a + pad_rank) * TILE_SUBLANES).astype(jnp.int32)
    n_used = (pend[-1] // bm).astype(jnp.int32).reshape(1)
    return block_e, n_used, src.reshape(n_blocks, 1, bm), dst.reshape(n_blocks, 1, bm), a + N_EXPERTS * bm


def _rope_tables(seq):
    rows = seq // GRID_W
    row = jnp.repeat(jnp.arange(rows, dtype=F32), GRID_W)
    col = jnp.tile(jnp.arange(GRID_W, dtype=F32), rows)
    half = HEAD_DIM // 2
    inv = AXIAL_THETA ** (-jnp.arange(0, half, 2, dtype=F32) / half)
    ang = jnp.concatenate([row[:, None] * inv, col[:, None] * inv], axis=-1)
    ang = jnp.concatenate([ang, ang], axis=-1).T
    sign_a = jnp.where(jnp.arange(HEAD_DIM) < half, -1.0, 1.0).astype(F32)[:, None]
    tt = jnp.arange(seq, dtype=F32)
    invb = ROPE_THETA ** (-jnp.arange(0, PARTIAL_ROT, 2, dtype=F32) / PARTIAL_ROT)
    angb = tt[:, None] * invb
    angb = jnp.concatenate([angb, angb], axis=-1).T
    sign_b = jnp.where(jnp.arange(PARTIAL_ROT) < PARTIAL_ROT // 2, -1.0, 1.0).astype(F32)[:, None]
    return jnp.cos(ang), jnp.sin(ang) * sign_a, jnp.cos(angb), jnp.sin(angb) * sign_b


def _trunk(x, p):
    bsz, s, d = x.shape
    t = bsz * s
    qa, qb, va, vb, ka, kb = _in_projection(x, p["g_attn"], p["w_in_t"], _rope_tables(s), p["qg"], p["kg"])
    oa = _attention_a(qa, ka, va)
    ob = _attention_b(p["lam"], qb, kb, vb, p["subln"])
    x2, xn, te, tg = _out_projection(oa, ob, x, p["wa"], p["wb"], p["g_ffn"], p["rw_t"], p["rb"])
    top_e = jnp.swapaxes(te, 1, 2).reshape(t, TOP_K)
    gates = jnp.swapaxes(tg, 1, 2).reshape(t, TOP_K)
    block_e, n_used, src_tok, dst_row, n_out_rows = _route(top_e)
    y = _moe_experts(block_e, n_used, src_tok, dst_row, n_out_rows, xn,
                     p["wg"], p["bg"], p["wu"], p["bu"], p["wd"], p["bd"])
    out = _combine(y, x2.reshape(t, d), gates, p["g_final"])
    return out.reshape(bsz, s, d)


def kernel(x_prompt, x_sample, attn_norm, w_in, a_q_norm, a_k_norm, b_lambda_q1, b_lambda_k1, b_lambda_q2,
           b_lambda_k2, b_subln, w_out, ffn_norm, router_w, router_b, w_gate, b_gate, w_up, b_up, w_down,
           b_down, final_norm):
    tm = TOKEN_TILE
    lam = (jnp.exp(jnp.sum(b_lambda_q1[0].astype(F32) * b_lambda_k1[0].astype(F32)))
           - jnp.exp(jnp.sum(b_lambda_q2[0].astype(F32) * b_lambda_k2[0].astype(F32)))
           + LAMBDA_INIT)
    p = {
        "g_attn": attn_norm[0].reshape(1, D_MODEL),
        "w_in_t": w_in[0].T.astype(BF16),
        "qg": jnp.broadcast_to(a_q_norm[0][:, None], (HEAD_DIM, tm)),
        "kg": jnp.broadcast_to(a_k_norm[0][:, None], (HEAD_DIM, tm)),
        "lam": lam.reshape(1).astype(F32),
        "subln": jnp.broadcast_to(b_subln[0][:, None], (B_V_DIM, B_Q_BLOCK)),
        "wa": w_out[0, :A_Q_W].astype(BF16),
        "wb": w_out[0, A_Q_W:].astype(BF16),
        "g_ffn": ffn_norm[0].reshape(1, D_MODEL),
        "rw_t": router_w[0].T,
        "rb": jnp.broadcast_to(router_b[0][:, None], (N_EXPERTS, tm)),
        "wg": w_gate[0].astype(BF16),
        "bg": b_gate[0].reshape(N_EXPERTS, 1, -1),
        "wu": w_up[0].astype(BF16),
        "bu": b_up[0].reshape(N_EXPERTS, 1, -1),
        "wd": w_down[0].astype(BF16),
        "bd": b_down[0].reshape(N_EXPERTS, 1, -1),
        "g_final": final_norm.reshape(1, D_MODEL),
    }
    return _trunk(x_prompt, p), _trunk(x_sample, p)
```

```python
import functools
import math

import jax
import jax.numpy as jnp
from jax import lax
from jax.experimental import pallas as pl
from jax.experimental.pallas import tpu as pltpu

F32 = jnp.float32
BF16 = jnp.bfloat16

D_MODEL = 1024
HEAD_DIM = 64
A_HEADS = 8
A_KV_HEADS = 2
A_GROUP = A_HEADS // A_KV_HEADS
B_HEADS = 4
B_V_DIM = 2 * HEAD_DIM
A_Q_W = A_HEADS * HEAD_DIM
A_KV_W = A_KV_HEADS * HEAD_DIM
B_QK_W = B_HEADS * 2 * HEAD_DIM
B_V_W = B_HEADS * B_V_DIM
IN_W = A_Q_W + 2 * A_KV_W + 2 * B_QK_W + B_V_W
GRID_W = 64
AXIAL_THETA = 10000.0
ROPE_THETA = 500000.0
PARTIAL_ROT = HEAD_DIM // 4
N_EXPERTS = 32
TOP_K = 4
SWIGLU_LIMIT = 7.0
SWIGLU_ALPHA = 1.702
EPS = 1e-6
LAMBDA_INIT = 0.8 - 0.6 * math.exp(-0.3 * 0)
Q_PRESCALE = HEAD_DIM ** -0.5 * math.log2(math.e)

O_AQ = 0
O_AK = O_AQ + A_Q_W
O_AV = O_AK + A_KV_W
O_BQ = O_AV + A_KV_W
O_BK = O_BQ + B_QK_W
O_BV = O_BK + B_QK_W

LANES = 128
TILE_SUBLANES = 8
BF16_SUBLANES = 16
ONES_ROWS = BF16_SUBLANES
VA_ROWS = HEAD_DIM + ONES_ROWS
VB_ROWS = B_V_DIM + ONES_ROWS
TOKEN_TILE = 512
A_Q_BLOCK = 256
B_Q_BLOCK = 1024
KV_CHUNK = 256
CHUNKS_PER_ITER = 4
COL_GROUP = 256
MOE_ROWS = 256
COMBINE_TOKENS = 256
VMEM_LIMIT = 48 * 1024 * 1024


def _cparams(sem):
    return pltpu.CompilerParams(dimension_semantics=sem, vmem_limit_bytes=VMEM_LIMIT)


def _inproj_kernel(x_ref, g_ref, w_ref, cosa_ref, sina_ref, cosb_ref, sinb_ref, qg_ref, kg_ref,
                   qa_ref, qb_ref, va_ref, vb_ref, ka_ref, kb_ref):
    x = x_ref[0]
    tm = x.shape[0]
    ms = jnp.mean(x * x, axis=-1, keepdims=True)
    n = (x * lax.rsqrt(ms + EPS) * g_ref[...]).astype(BF16)
    pt = lax.dot_general(w_ref[...], n, (((1,), (1,)), ((), ())), preferred_element_type=F32)

    cosa = cosa_ref[...]
    sina = sina_ref[...]
    cosb = cosb_ref[...]
    sinb = sinb_ref[...]
    half = HEAD_DIM // 2
    hrot = PARTIAL_ROT // 2

    def norm_rope_a(t, gain):
        y = t * lax.rsqrt(jnp.mean(t * t, axis=0, keepdims=True) + EPS) * gain
        rot = jnp.concatenate([y[half:], y[:half]], axis=0)
        return y * cosa + rot * sina

    def rope_b(t):
        head = t[:PARTIAL_ROT]
        rot = jnp.concatenate([head[hrot:], head[:hrot]], axis=0)
        return jnp.concatenate([head * cosb + rot * sinb, t[PARTIAL_ROT:]], axis=0)

    qg = qg_ref[...]
    kg = kg_ref[...]
    ones = jnp.ones((ONES_ROWS, tm), BF16)
    for h in range(A_HEADS):
        t = pt[O_AQ + h * HEAD_DIM:O_AQ + (h + 1) * HEAD_DIM]
        qa_ref[0, h * HEAD_DIM:(h + 1) * HEAD_DIM, :] = (norm_rope_a(t, qg) * Q_PRESCALE).astype(BF16)
    for h in range(A_KV_HEADS):
        va_ref[0, h * VA_ROWS:h * VA_ROWS + HEAD_DIM, :] = (
            pt[O_AV + h * HEAD_DIM:O_AV + (h + 1) * HEAD_DIM].astype(BF16))
        va_ref[0, h * VA_ROWS + HEAD_DIM:(h + 1) * VA_ROWS, :] = ones
    for j in range(B_QK_W // HEAD_DIM):
        t = pt[O_BQ + j * HEAD_DIM:O_BQ + (j + 1) * HEAD_DIM]
        qb_ref[0, j * HEAD_DIM:(j + 1) * HEAD_DIM, :] = (rope_b(t) * Q_PRESCALE).astype(BF16)
    for h in range(B_HEADS):
        vb_ref[0, h * VB_ROWS:h * VB_ROWS + B_V_DIM, :] = (
            pt[O_BV + h * B_V_DIM:O_BV + (h + 1) * B_V_DIM].astype(BF16))
        vb_ref[0, h * VB_ROWS + B_V_DIM:(h + 1) * VB_ROWS, :] = ones

    ka = jnp.concatenate([norm_rope_a(pt[O_AK + h * HEAD_DIM:O_AK + (h + 1) * HEAD_DIM], kg)
                          for h in range(A_KV_HEADS)], axis=0)
    ka_ref[0] = ka.T.astype(BF16)
    for h in range(B_HEADS):
        kb = jnp.concatenate([rope_b(pt[O_BK + (2 * h + c) * HEAD_DIM:O_BK + (2 * h + c + 1) * HEAD_DIM])
                              for c in range(2)], axis=0)
        kb_ref[0, h] = kb.T.astype(BF16)


def _in_projection(x, g_attn, w_in_t, tabs, qg, kg):
    bsz, s, d = x.shape
    tm = TOKEN_TILE
    cosa, sina, cosb, sinb = tabs
    const = lambda shape: pl.BlockSpec(shape, lambda b, i: (0,) * len(shape))
    rows = lambda r: pl.BlockSpec((1, r, tm), lambda b, i: (b, 0, i))
    return pl.pallas_call(
        _inproj_kernel,
        grid=(bsz, s // tm),
        in_specs=[
            pl.BlockSpec((1, tm, d), lambda b, i: (b, i, 0)),
            const((1, d)),
            const((IN_W, d)),
            pl.BlockSpec((HEAD_DIM, tm), lambda b, i: (0, i)),
            pl.BlockSpec((HEAD_DIM, tm), lambda b, i: (0, i)),
            pl.BlockSpec((PARTIAL_ROT, tm), lambda b, i: (0, i)),
            pl.BlockSpec((PARTIAL_ROT, tm), lambda b, i: (0, i)),
            const((HEAD_DIM, tm)),
            const((HEAD_DIM, tm)),
        ],
        out_specs=[
            rows(A_Q_W), rows(B_QK_W), rows(A_KV_HEADS * VA_ROWS), rows(B_HEADS * VB_ROWS),
            pl.BlockSpec((1, tm, LANES), lambda b, i: (b, i, 0)),
            pl.BlockSpec((1, B_HEADS, tm, LANES), lambda b, i: (b, 0, i, 0)),
        ],
        out_shape=[
            jax.ShapeDtypeStruct((bsz, A_Q_W, s), BF16),
            jax.ShapeDtypeStruct((bsz, B_QK_W, s), BF16),
            jax.ShapeDtypeStruct((bsz, A_KV_HEADS * VA_ROWS, s), BF16),
            jax.ShapeDtypeStruct((bsz, B_HEADS * VB_ROWS, s), BF16),
            jax.ShapeDtypeStruct((bsz, s, LANES), BF16),
            jax.ShapeDtypeStruct((bsz, B_HEADS, s, LANES), BF16),
        ],
        compiler_params=_cparams(("parallel", "parallel")),
        name="in_projection",
    )(x, g_attn, w_in_t, cosa, sina, cosb, sinb, qg, kg)


def _column_softmax_attention(k_ref, rhs_ref, v_ref, v_rows, acc_ref, m_ref, s_refs, seq):
    n = rhs_ref.shape[1]
    bk = KV_CHUNK
    gw = COL_GROUP
    n_groups = n // gw
    n_iters = seq // (CHUNKS_PER_ITER * bk)

    def scores(c, s_ref, g):
        off = pl.multiple_of(c * bk, bk)
        s_ref[:, g * gw:(g + 1) * gw] = jnp.dot(k_ref[pl.ds(off, bk), :], rhs_ref[:, g * gw:(g + 1) * gw],
                                                preferred_element_type=F32)

    def softmax_pv(c, s_ref, g):
        off = pl.multiple_of(c * bk, bk)
        cols = slice(g * gw, (g + 1) * gw)
        m_old = m_ref[:, cols]
        m_parts, p_parts = [], []
        for j in range(gw // LANES):
            lo = g * gw + j * LANES
            sj = s_ref[:, lo:lo + LANES]
            mj = jnp.maximum(m_old[:, j * LANES:(j + 1) * LANES], jnp.max(sj, axis=0, keepdims=True))
            p_parts.append(jnp.exp2((sj - mj).astype(BF16)))
            m_parts.append(mj)
        m_new = jnp.concatenate(m_parts, axis=1)
        m_ref[:, cols] = m_new
        alpha = jnp.exp2(m_old - m_new)
        rl, rh = v_rows(g)
        pv = jnp.dot(v_ref[rl:rh, pl.ds(off, bk)], jnp.concatenate(p_parts, axis=1),
                     preferred_element_type=F32)
        acc_ref[:, cols] = alpha * acc_ref[:, cols] + pv

    def half(c, cur, nxt, issue_next):
        for g in range(n_groups):
            if issue_next:
                scores(c + 1, nxt, g)
            softmax_pv(c, cur, g)

    def chunks(j, last):
        for u in range(CHUNKS_PER_ITER):
            half(j * CHUNKS_PER_ITER + u, s_refs[u % 2], s_refs[(u + 1) % 2],
                 not (last and u == CHUNKS_PER_ITER - 1))

    acc_ref[...] = jnp.zeros_like(acc_ref)
    m_ref[...] = jnp.full(m_ref.shape, -jnp.inf, F32)
    for g in range(n_groups):
        scores(0, s_refs[0], g)

    def body(j, carry):
        chunks(j, False)
        return carry

    lax.fori_loop(0, n_iters - 1, body, 0)
    chunks(n_iters - 1, True)


def _attn_a_kernel(q_ref, k_ref, v_ref, o_ref, rhs_ref, acc_ref, m_ref, s0, s1, *, seq):
    bq = A_Q_BLOCK
    half_n = A_GROUP * bq
    rhs_ref[...] = jnp.zeros_like(rhs_ref)
    for h in range(A_HEADS):
        kv = h // A_GROUP
        rhs_ref[kv * HEAD_DIM:(kv + 1) * HEAD_DIM, h * bq:(h + 1) * bq] = q_ref[0, h * HEAD_DIM:(h + 1) * HEAD_DIM, :]
    v_rows = lambda g: ((g * COL_GROUP // half_n) * VA_ROWS, (g * COL_GROUP // half_n + 1) * VA_ROWS)
    _column_softmax_attention(k_ref.at[0], rhs_ref, v_ref.at[0], v_rows, acc_ref, m_ref, (s0, s1), seq)
    acc = acc_ref[...]
    o = acc[:HEAD_DIM] * (1.0 / acc[HEAD_DIM:HEAD_DIM + 1])
    for h in range(A_HEADS):
        o_ref[0, h * HEAD_DIM:(h + 1) * HEAD_DIM, :] = o[:, h * bq:(h + 1) * bq].astype(BF16)


def _attention_scratch(acc_rows, n):
    return [pltpu.VMEM((LANES, n), BF16), pltpu.VMEM((acc_rows, n), F32), pltpu.VMEM((1, n), F32),
            pltpu.VMEM((KV_CHUNK, n), F32), pltpu.VMEM((KV_CHUNK, n), F32)]


def _attention_a(qa, ka, va):
    bsz, _, s = qa.shape
    bq = A_Q_BLOCK
    assert s % (CHUNKS_PER_ITER * KV_CHUNK) == 0 and s % bq == 0
    n = A_HEADS * bq
    return pl.pallas_call(
        functools.partial(_attn_a_kernel, seq=s),
        grid=(bsz, s // bq),
        in_specs=[
            pl.BlockSpec((1, A_Q_W, bq), lambda b, i: (b, 0, i)),
            pl.BlockSpec((1, s, LANES), lambda b, i: (b, 0, 0)),
            pl.BlockSpec((1, A_KV_HEADS * VA_ROWS, s), lambda b, i: (b, 0, 0)),
        ],
        out_specs=pl.BlockSpec((1, A_Q_W, bq), lambda b, i: (b, 0, i)),
        out_shape=jax.ShapeDtypeStruct((bsz, A_Q_W, s), BF16),
        scratch_shapes=_attention_scratch(VA_ROWS, n),
        compiler_params=_cparams(("parallel", "parallel")),
        name="attention_a",
    )(qa, ka, va)


def _attn_b_kernel(lam_ref, q_ref, k_ref, v_ref, sg_ref, o_ref, rhs_ref, acc, m_ref, s0, s1, *, seq):
    bq = B_Q_BLOCK
    rhs_ref[...] = jnp.zeros_like(rhs_ref)
    rhs_ref[:HEAD_DIM, :bq] = q_ref[0, :HEAD_DIM, :]
    rhs_ref[HEAD_DIM:, bq:] = q_ref[0, HEAD_DIM:, :]
    _column_softmax_attention(k_ref.at[0, 0], rhs_ref, v_ref.at[0], lambda g: (0, VB_ROWS), acc, m_ref,
                              (s0, s1), seq)
    a = acc[...]
    o = a[:B_V_DIM] * (1.0 / a[B_V_DIM:B_V_DIM + 1])
    o = o[:, :bq] - lam_ref[0] * o[:, bq:]
    o = o * lax.rsqrt(jnp.mean(o * o, axis=0, keepdims=True) + EPS) * sg_ref[...]
    o_ref[0] = (o * (1.0 - LAMBDA_INIT)).astype(BF16)


def _attention_b(lam, qb, kb, vb, subln):
    bsz, _, s = qb.shape
    bq = B_Q_BLOCK
    assert s % (CHUNKS_PER_ITER * KV_CHUNK) == 0 and s % bq == 0
    n = 2 * bq
    return pl.pallas_call(
        functools.partial(_attn_b_kernel, seq=s),
        grid=(bsz, B_HEADS, s // bq),
        in_specs=[
            pl.BlockSpec(memory_space=pltpu.SMEM),
            pl.BlockSpec((1, B_V_DIM, bq), lambda b, h, i: (b, h, i)),
            pl.BlockSpec((1, 1, s, LANES), lambda b, h, i: (b, h, 0, 0)),
            pl.BlockSpec((1, VB_ROWS, s), lambda b, h, i: (b, h, 0)),
            pl.BlockSpec((B_V_DIM, bq), lambda b, h, i: (0, 0)),
        ],
        out_specs=pl.BlockSpec((1, B_V_DIM, bq), lambda b, h, i: (b, h, i)),
        out_shape=jax.ShapeDtypeStruct((bsz, B_V_W, s), BF16),
        scratch_shapes=_attention_scratch(VB_ROWS, n),
        compiler_params=_cparams(("parallel", "parallel", "parallel")),
        name="attention_b",
    )(lam, qb, kb, vb, subln)


def _store_token_tiles(ref, val):
    n = val.shape[0]
    for j in range(TILE_SUBLANES):
        ref[pl.ds(j, n, stride=TILE_SUBLANES), :] = val[:, j * LANES:(j + 1) * LANES]


def _load_token_tiles(ref, n):
    return jnp.concatenate([ref[pl.ds(j, n, stride=TILE_SUBLANES), :] for j in range(TILE_SUBLANES)], axis=1)


def _outproj_kernel(a_ref, b_ref, x_ref, wa_ref, wb_ref, g_ref, rw_ref, rb_ref,
                    x2_ref, xn_ref, te_ref, tg_ref):
    tn = (((0,), (0,)), ((), ()))
    y = (x_ref[0]
         + lax.dot_general(a_ref[0], wa_ref[...], tn, preferred_element_type=F32)
         + lax.dot_general(b_ref[0], wb_ref[...], tn, preferred_element_type=F32))
    x2_ref[0] = y
    xn = y * lax.rsqrt(jnp.mean(y * y, axis=-1, keepdims=True) + EPS) * g_ref[...]
    _store_token_tiles(xn_ref, xn)
    logits = lax.dot_general(rw_ref[...], xn, (((1,), (1,)), ((), ())),
                             precision=lax.Precision.HIGHEST, preferred_element_type=F32) + rb_ref[...]
    iota = lax.broadcasted_iota(jnp.int32, logits.shape, 0)
    work = logits
    vals, idxs = [], []
    for _ in range(TOP_K):
        mx = jnp.max(work, axis=0, keepdims=True)
        idx = jnp.min(jnp.where(work == mx, iota, N_EXPERTS), axis=0, keepdims=True)
        vals.append(mx)
        idxs.append(idx)
        work = jnp.where(iota == idx, -jnp.inf, work)
    ex = [jnp.exp(v - vals[0]) for v in vals]
    inv = 1.0 / (ex[0] + ex[1] + ex[2] + ex[3])
    te_ref[0] = jnp.concatenate(idxs, axis=0)
    tg_ref[0] = jnp.concatenate([e * inv for e in ex], axis=0)


def _out_projection(oa, ob, x, wa, wb, g_ffn, rw_t, rb):
    bsz, s, d = x.shape
    tm = TOKEN_TILE
    const = lambda shape: pl.BlockSpec(shape, lambda b, i: (0,) * len(shape))
    tok = pl.BlockSpec((1, tm, d), lambda b, i: (b, i, 0))
    sel = pl.BlockSpec((1, TOP_K, tm), lambda b, i: (b, 0, i))
    return pl.pallas_call(
        _outproj_kernel,
        grid=(bsz, s // tm),
        in_specs=[
            pl.BlockSpec((1, A_Q_W, tm), lambda b, i: (b, 0, i)),
            pl.BlockSpec((1, B_V_W, tm), lambda b, i: (b, 0, i)),
            tok,
            const((A_Q_W, d)),
            const((B_V_W, d)),
            const((1, d)),
            const((N_EXPERTS, d)),
            const((N_EXPERTS, tm)),
        ],
        out_specs=[tok,
                   pl.BlockSpec((tm * TILE_SUBLANES, LANES), lambda b, i: (b * (s // tm) + i, 0)),
                   sel, sel],
        out_shape=[
            jax.ShapeDtypeStruct((bsz, s, d), F32),
            jax.ShapeDtypeStruct((bsz * s * TILE_SUBLANES, LANES), F32),
            jax.ShapeDtypeStruct((bsz, TOP_K, s), jnp.int32),
            jax.ShapeDtypeStruct((bsz, TOP_K, s), F32),
        ],
        compiler_params=_cparams(("parallel", "parallel")),
        name="out_projection_router",
    )(oa, ob, x, wa, wb, g_ffn, rw_t, rb)


def _tile_rows(r):
    return pl.ds(r * TILE_SUBLANES, TILE_SUBLANES)


def _gather_row(idx_ref, r, src_hbm, buf, sem):
    src = pl.ds(pl.multiple_of(idx_ref[0, 0, r], TILE_SUBLANES), TILE_SUBLANES)
    return pltpu.make_async_copy(src_hbm.at[src, :], buf.at[_tile_rows(r), :], sem)


def _scatter_row(idx_ref, r, buf, dst_hbm, sem):
    dst = pl.ds(pl.multiple_of(idx_ref[0, 0, r], TILE_SUBLANES), TILE_SUBLANES)
    return pltpu.make_async_copy(buf.at[_tile_rows(r), :], dst_hbm.at[dst, :], sem)


def _moe_kernel(be_ref, nb_ref, tok_first, tok_nxt, dst_cur, xn_hbm, wg_ref, bg_ref, wu_ref, bu_ref,
                wd_ref, bd_ref, y_hbm, xbuf0, xbuf1, ybuf0, ybuf1, gsem, ssem, *, pad_base):
    i = pl.program_id(0)
    n_used = nb_ref[0]
    rows = MOE_ROWS
    tile_rows = rows * TILE_SUBLANES
    xbufs = (xbuf0, xbuf1)
    ybufs = (ybuf0, ybuf1)

    @pl.when(i == 0)
    def _():
        ybuf0[...] = jnp.zeros_like(ybuf0)
        fills = [pltpu.make_async_copy(ybuf0, y_hbm.at[pl.ds(pad_base + j * tile_rows, tile_rows), :],
                                       ssem.at[0]) for j in range(N_EXPERTS)]
        for f in fills:
            f.start()
        for f in fills:
            f.wait()
        for r in range(rows):
            _gather_row(tok_first, r, xn_hbm, xbuf0, gsem.at[0]).start(priority=r % 2)

    def step(slot):
        xbuf, ybuf = xbufs[slot], ybufs[slot]

        @pl.when(i < n_used)
        def _():
            for r in range(rows):
                _gather_row(tok_nxt, r, xn_hbm, xbufs[1 - slot], gsem.at[1 - slot]).start(priority=r % 2)

        @pl.when(i <= n_used)
        def _():
            pltpu.make_async_copy(xn_hbm.at[pl.ds(0, tile_rows), :], xbuf, gsem.at[slot]).wait()

        @pl.when(jnp.logical_and(i >= 2, i - 2 < n_used))
        def _():
            pltpu.make_async_copy(ybuf, y_hbm.at[pl.ds(0, tile_rows), :], ssem.at[slot]).wait()

        @pl.when(i < n_used)
        def _():
            xb = _load_token_tiles(xbuf, rows).astype(BF16)
            g = jnp.dot(xb, wg_ref[0], preferred_element_type=F32) + bg_ref[0]
            u = jnp.dot(xb, wu_ref[0], preferred_element_type=F32) + bu_ref[0]
            g = jnp.minimum(g, SWIGLU_LIMIT)
            u = jnp.clip(u, -SWIGLU_LIMIT, SWIGLU_LIMIT)
            h = g * jax.nn.sigmoid(g * SWIGLU_ALPHA) * (u + 1.0)
            y = jnp.dot(h.astype(BF16), wd_ref[0], preferred_element_type=F32) + bd_ref[0]
            _store_token_tiles(ybuf, y)
            for r in range(rows):
                _scatter_row(dst_cur, r, ybuf, y_hbm, ssem.at[slot]).start(priority=r % 2)

    for slot in range(2):
        pl.when(lax.rem(i, 2) == slot)(functools.partial(step, slot))


def _moe_experts(block_e, n_used, src_tok, dst_row, n_out_rows, xn, wg, bg, wu, bu, wd, bd):
    d = D_MODEL
    t = xn.shape[0] // TILE_SUBLANES
    bm = MOE_ROWS
    n_blocks = src_tok.shape[0]
    dff = wg.shape[2]
    last = n_blocks - 1
    n_steps = n_blocks + 2
    blk = lambda i: jnp.minimum(i, last)
    wspec = lambda k, n: pl.BlockSpec((1, k, n), lambda i, be, nb: (be[blk(i)], 0, 0))
    idx = lambda fn: pl.BlockSpec((1, 1, bm), lambda i, be, nb: (fn(i), 0, 0), memory_space=pltpu.SMEM)
    grid_spec = pltpu.PrefetchScalarGridSpec(
        num_scalar_prefetch=2,
        grid=(n_steps,),
        in_specs=[
            idx(lambda i: 0), idx(lambda i: blk(i + 1)), idx(blk),
            pl.BlockSpec(memory_space=pl.ANY),
            wspec(d, dff), wspec(1, dff), wspec(d, dff), wspec(1, dff), wspec(dff, d), wspec(1, d),
        ],
        out_specs=pl.BlockSpec(memory_space=pl.ANY),
        scratch_shapes=[pltpu.VMEM((bm * TILE_SUBLANES, LANES), F32)] * 4
                       + [pltpu.SemaphoreType.DMA((2,)), pltpu.SemaphoreType.DMA((2,))],
    )
    return pl.pallas_call(
        functools.partial(_moe_kernel, pad_base=t * TOP_K * TILE_SUBLANES),
        grid_spec=grid_spec,
        out_shape=jax.ShapeDtypeStruct((n_out_rows * TILE_SUBLANES, LANES), F32),
        compiler_params=_cparams(("arbitrary",)),
        name="moe_experts",
    )(block_e, n_used, src_tok, src_tok, dst_row, xn, wg, bg, wu, bu, wd, bd)


def _combine_kernel(y0_ref, y1_ref, y2_ref, y3_ref, x2_ref, gate_ref, g_ref, o_ref):
    tc = x2_ref.shape[0]
    gates = gate_ref[...]
    y = x2_ref[...]
    for k, y_ref in enumerate((y0_ref, y1_ref, y2_ref, y3_ref)):
        y = y + gates[:, k:k + 1] * _load_token_tiles(y_ref, tc)
    o_ref[...] = y * lax.rsqrt(jnp.mean(y * y, axis=-1, keepdims=True) + EPS) * g_ref[...]


def _combine(y, x2, gates, g_final):
    t, d = x2.shape
    tc = COMBINE_TOKENS
    n = t // tc
    yspec = lambda k: pl.BlockSpec((tc * TILE_SUBLANES, LANES), lambda i: (k * n + i, 0))
    return pl.pallas_call(
        _combine_kernel,
        grid=(n,),
        in_specs=[
            yspec(0), yspec(1), yspec(2), yspec(3),
            pl.BlockSpec((tc, d), lambda i: (i, 0)),
            pl.BlockSpec((tc, TOP_K), lambda i: (i, 0)),
            pl.BlockSpec((1, d), lambda i: (0, 0)),
        ],
        out_specs=pl.BlockSpec((tc, d), lambda i: (i, 0)),
        out_shape=jax.ShapeDtypeStruct((t, d), F32),
        compiler_params=_cparams(("parallel",)),
        name="combine_final_norm",
    )(y, y, y, y, x2, gates, g_final)


def _route(top_e):
    t = top_e.shape[0]
    a = t * TOP_K
    bm = MOE_ROWS
    n_blocks = a // bm + N_EXPERTS
    flat_e = top_e.reshape(-1)
    order = jnp.argsort(flat_e, stable=True).astype(jnp.int32)
    counts = jnp.bincount(flat_e, length=N_EXPERTS).astype(jnp.int32)
    padded = ((counts + bm - 1) // bm) * bm
    start = jnp.cumsum(counts) - counts
    pend = jnp.cumsum(padded)
    pstart = pend - padded
    block_e = jnp.minimum(jnp.searchsorted(pend, jnp.arange(n_blocks, dtype=jnp.int32) * bm, side='right'),
                          N_EXPERTS - 1).astype(jnp.int32)
    rows = jnp.arange(n_blocks * bm, dtype=jnp.int32)
    row_e = jnp.repeat(block_e, bm)
    within = rows - pstart[row_e]
    valid = within < counts[row_e]
    assign = order[jnp.clip(start[row_e] + within, 0, a - 1)]
    pad_rank = jnp.cumsum(jnp.logical_not(valid).astype(jnp.int32)) - 1
    tok = assign // TOP_K
    src = (jnp.where(valid, tok, 0) * TILE_SUBLANES).astype(jnp.int32)
    dst = (jnp.where(valid, (assign % TOP_K) * t + tok, a + pad_rank) * TILE_SUBLANES).astype(jnp.int32)
    n_used = (pend[-1] // bm).astype(jnp.int32).reshape(1)
    return block_e, n_used, src.reshape(n_blocks, 1, bm), dst.reshape(n_blocks, 1, bm), a + N_EXPERTS * bm


def _rope_tables(seq):
    rows = seq // GRID_W
    row = jnp.repeat(jnp.arange(rows, dtype=F32), GRID_W)
    col = jnp.tile(jnp.arange(GRID_W, dtype=F32), rows)
    half = HEAD_DIM // 2
    inv = AXIAL_THETA ** (-jnp.arange(0, half, 2, dtype=F32) / half)
    ang = jnp.concatenate([row[:, None] * inv, col[:, None] * inv], axis=-1)
    ang = jnp.concatenate([ang, ang], axis=-1).T
    sign_a = jnp.where(jnp.arange(HEAD_DIM) < half, -1.0, 1.0).astype(F32)[:, None]
    tt = jnp.arange(seq, dtype=F32)
    invb = ROPE_THETA ** (-jnp.arange(0, PARTIAL_ROT, 2, dtype=F32) / PARTIAL_ROT)
    angb = tt[:, None] * invb
    angb = jnp.concatenate([angb, angb], axis=-1).T
    sign_b = jnp.where(jnp.arange(PARTIAL_ROT) < PARTIAL_ROT // 2, -1.0, 1.0).astype(F32)[:, None]
    return jnp.cos(ang), jnp.sin(ang) * sign_a, jnp.cos(angb), jnp.sin(angb) * sign_b


def _trunk(x, p):
    bsz, s, d = x.shape
    t = bsz * s
    qa, qb, va, vb, ka, kb = _in_projection(x, p["g_attn"], p["w_in_t"], _rope_tables(s), p["qg"], p["kg"])
    oa = _attention_a(qa, ka, va)
    ob = _attention_b(p["lam"], qb, kb, vb, p["subln"])
    x2, xn, te, tg = _out_projection(oa, ob, x, p["wa"], p["wb"], p["g_ffn"], p["rw_t"], p["rb"])
    top_e = jnp.swapaxes(te, 1, 2).reshape(t, TOP_K)
    gates = jnp.swapaxes(tg, 1, 2).reshape(t, TOP_K)
    block_e, n_used, src_tok, dst_row, n_out_rows = _route(top_e)
    y = _moe_experts(block_e, n_used, src_tok, dst_row, n_out_rows, xn,
                     p["wg"], p["bg"], p["wu"], p["bu"], p["wd"], p["bd"])
    out = _combine(y, x2.reshape(t, d), gates, p["g_final"])
    return out.reshape(bsz, s, d)


def kernel(x_prompt, x_sample, attn_norm, w_in, a_q_norm, a_k_norm, b_lambda_q1, b_lambda_k1, b_lambda_q2,
           b_lambda_k2, b_subln, w_out, ffn_norm, router_w, router_b, w_gate, b_gate, w_up, b_up, w_down,
           b_down, final_norm):
    tm = TOKEN_TILE
    lam = (jnp.exp(jnp.sum(b_lambda_q1[0].astype(F32) * b_lambda_k1[0].astype(F32)))
           - jnp.exp(jnp.sum(b_lambda_q2[0].astype(F32) * b_lambda_k2[0].astype(F32)))
           + LAMBDA_INIT)
    p = {
        "g_attn": attn_norm[0].reshape(1, D_MODEL),
        "w_in_t": w_in[0].T.astype(BF16),
        "qg": jnp.broadcast_to(a_q_norm[0][:, None], (HEAD_DIM, tm)),
        "kg": jnp.broadcast_to(a_k_norm[0][:, None], (HEAD_DIM, tm)),
        "lam": lam.reshape(1).astype(F32),
        "subln": jnp.broadcast_to(b_subln[0][:, None], (B_V_DIM, B_Q_BLOCK)),
        "wa": w_out[0, :A_Q_W].astype(BF16),
        "wb": w_out[0, A_Q_W:].astype(BF16),
        "g_ffn": ffn_norm[0].reshape(1, D_MODEL),
        "rw_t": router_w[0].T,
        "rb": jnp.broadcast_to(router_b[0][:, None], (N_EXPERTS, tm)),
        "wg": w_gate[0].astype(BF16),
        "bg": b_gate[0].reshape(N_EXPERTS, 1, -1),
        "wu": w_up[0].astype(BF16),
        "bu": b_up[0].reshape(N_EXPERTS, 1, -1),
        "wd": w_down[0].astype(BF16),
        "bd": b_down[0].reshape(N_EXPERTS, 1, -1),
        "g_final": final_norm.reshape(1, D_MODEL),
    }
    return _trunk(x_prompt, p), _trunk(x_sample, p)
```

```python
import functools
import math

import jax
import jax.numpy as jnp
from jax import lax
from jax.experimental import pallas as pl
from jax.experimental.pallas import tpu as pltpu

F32 = jnp.float32
BF16 = jnp.bfloat16

D_MODEL = 1024
HEAD_DIM = 64
A_HEADS = 8
A_KV_HEADS = 2
A_GROUP = A_HEADS // A_KV_HEADS
B_HEADS = 4
B_V_DIM = 2 * HEAD_DIM
A_Q_W = A_HEADS * HEAD_DIM
A_KV_W = A_KV_HEADS * HEAD_DIM
B_QK_W = B_HEADS * 2 * HEAD_DIM
B_V_W = B_HEADS * B_V_DIM
IN_W = A_Q_W + 2 * A_KV_W + 2 * B_QK_W + B_V_W
GRID_W = 64
AXIAL_THETA = 10000.0
ROPE_THETA = 500000.0
PARTIAL_ROT = HEAD_DIM // 4
N_EXPERTS = 32
TOP_K = 4
SWIGLU_LIMIT = 7.0
SWIGLU_ALPHA = 1.702
EPS = 1e-6
LAMBDA_INIT = 0.8 - 0.6 * math.exp(-0.3 * 0)
Q_PRESCALE = HEAD_DIM ** -0.5 * math.log2(math.e)

O_AQ = 0
O_AK = O_AQ + A_Q_W
O_AV = O_AK + A_KV_W
O_BQ = O_AV + A_KV_W
O_BK = O_BQ + B_QK_W
O_BV = O_BK + B_QK_W

LANES = 128
TILE_SUBLANES = 8
BF16_SUBLANES = 16
ONES_ROWS = BF16_SUBLANES
VA_ROWS = HEAD_DIM + ONES_ROWS
VB_ROWS = B_V_DIM + ONES_ROWS
TOKEN_TILE = 512
A_Q_BLOCK = 256
B_Q_BLOCK = 1024
KV_CHUNK = 256
CHUNKS_PER_ITER = 8
COL_GROUP = 256
MOE_ROWS = 256
K_SPLIT = 4
N_SPLIT = 4
COMBINE_TOKENS = 256
VMEM_LIMIT = 48 * 1024 * 1024


def _cparams(sem):
    return pltpu.CompilerParams(dimension_semantics=sem, vmem_limit_bytes=VMEM_LIMIT)


def _inproj_kernel(x_ref, g_ref, w_ref, cosa_ref, sina_ref, cosb_ref, sinb_ref, qg_ref, kg_ref,
                   qa_ref, qb_ref, va_ref, vb_ref, ka_ref, kb_ref):
    x = x_ref[0]
    tm = x.shape[0]
    ms = jnp.mean(x * x, axis=-1, keepdims=True)
    n = (x * lax.rsqrt(ms + EPS) * g_ref[...]).astype(BF16)
    pt = lax.dot_general(w_ref[...], n, (((1,), (1,)), ((), ())), preferred_element_type=F32)

    cosa = cosa_ref[...]
    sina = sina_ref[...]
    cosb = cosb_ref[...]
    sinb = sinb_ref[...]
    half = HEAD_DIM // 2
    hrot = PARTIAL_ROT // 2

    def norm_rope_a(t, gain):
        y = t * lax.rsqrt(jnp.mean(t * t, axis=0, keepdims=True) + EPS) * gain
        rot = jnp.concatenate([y[half:], y[:half]], axis=0)
        return y * cosa + rot * sina

    def rope_b(t):
        head = t[:PARTIAL_ROT]
        rot = jnp.concatenate([head[hrot:], head[:hrot]], axis=0)
        return jnp.concatenate([head * cosb + rot * sinb, t[PARTIAL_ROT:]], axis=0)

    qg = qg_ref[...]
    kg = kg_ref[...]
    ones = jnp.ones((ONES_ROWS, tm), BF16)
    for h in range(A_HEADS):
        t = pt[O_AQ + h * HEAD_DIM:O_AQ + (h + 1) * HEAD_DIM]
        qa_ref[0, h * HEAD_DIM:(h + 1) * HEAD_DIM, :] = (norm_rope_a(t, qg) * Q_PRESCALE).astype(BF16)
    for h in range(A_KV_HEADS):
        va_ref[0, h * VA_ROWS:h * VA_ROWS + HEAD_DIM, :] = (
            pt[O_AV + h * HEAD_DIM:O_AV + (h + 1) * HEAD_DIM].astype(BF16))
        va_ref[0, h * VA_ROWS + HEAD_DIM:(h + 1) * VA_ROWS, :] = ones
    for j in range(B_QK_W // HEAD_DIM):
        t = pt[O_BQ + j * HEAD_DIM:O_BQ + (j + 1) * HEAD_DIM]
        qb_ref[0, j * HEAD_DIM:(j + 1) * HEAD_DIM, :] = (rope_b(t) * Q_PRESCALE).astype(BF16)
    for h in range(B_HEADS):
        vb_ref[0, h * VB_ROWS:h * VB_ROWS + B_V_DIM, :] = (
            pt[O_BV + h * B_V_DIM:O_BV + (h + 1) * B_V_DIM].astype(BF16))
        vb_ref[0, h * VB_ROWS + B_V_DIM:(h + 1) * VB_ROWS, :] = ones

    ka = jnp.concatenate([norm_rope_a(pt[O_AK + h * HEAD_DIM:O_AK + (h + 1) * HEAD_DIM], kg)
                          for h in range(A_KV_HEADS)], axis=0)
    ka_ref[0] = ka.T.astype(BF16)
    for h in range(B_HEADS):
        kb = jnp.concatenate([rope_b(pt[O_BK + (2 * h + c) * HEAD_DIM:O_BK + (2 * h + c + 1) * HEAD_DIM])
                              for c in range(2)], axis=0)
        kb_ref[0, h] = kb.T.astype(BF16)


def _in_projection(x, g_attn, w_in_t, tabs, qg, kg):
    bsz, s, d = x.shape
    tm = TOKEN_TILE
    cosa, sina, cosb, sinb = tabs
    const = lambda shape: pl.BlockSpec(shape, lambda b, i: (0,) * len(shape))
    rows = lambda r: pl.BlockSpec((1, r, tm), lambda b, i: (b, 0, i))
    return pl.pallas_call(
        _inproj_kernel,
        grid=(bsz, s // tm),
        in_specs=[
            pl.BlockSpec((1, tm, d), lambda b, i: (b, i, 0)),
            const((1, d)),
            const((IN_W, d)),
            pl.BlockSpec((HEAD_DIM, tm), lambda b, i: (0, i)),
            pl.BlockSpec((HEAD_DIM, tm), lambda b, i: (0, i)),
            pl.BlockSpec((PARTIAL_ROT, tm), lambda b, i: (0, i)),
            pl.BlockSpec((PARTIAL_ROT, tm), lambda b, i: (0, i)),
            const((HEAD_DIM, tm)),
            const((HEAD_DIM, tm)),
        ],
        out_specs=[
            rows(A_Q_W), rows(B_QK_W), rows(A_KV_HEADS * VA_ROWS), rows(B_HEADS * VB_ROWS),
            pl.BlockSpec((1, tm, LANES), lambda b, i: (b, i, 0)),
            pl.BlockSpec((1, B_HEADS, tm, LANES), lambda b, i: (b, 0, i, 0)),
        ],
        out_shape=[
            jax.ShapeDtypeStruct((bsz, A_Q_W, s), BF16),
            jax.ShapeDtypeStruct((bsz, B_QK_W, s), BF16),
            jax.ShapeDtypeStruct((bsz, A_KV_HEADS * VA_ROWS, s), BF16),
            jax.ShapeDtypeStruct((bsz, B_HEADS * VB_ROWS, s), BF16),
            jax.ShapeDtypeStruct((bsz, s, LANES), BF16),
            jax.ShapeDtypeStruct((bsz, B_HEADS, s, LANES), BF16),
        ],
        compiler_params=_cparams(("parallel", "parallel")),
        name="in_projection",
    )(x, g_attn, w_in_t, cosa, sina, cosb, sinb, qg, kg)


def _column_softmax_attention(k_ref, rhs_ref, v_ref, v_rows, acc_ref, m_ref, s_refs, seq):
    n = rhs_ref.shape[1]
    bk = KV_CHUNK
    gw = COL_GROUP
    n_groups = n // gw
    n_iters = seq // (CHUNKS_PER_ITER * bk)

    def scores(c, s_ref, g):
        off = pl.multiple_of(c * bk, bk)
        s_ref[:, g * gw:(g + 1) * gw] = jnp.dot(k_ref[pl.ds(off, bk), :], rhs_ref[:, g * gw:(g + 1) * gw],
                                                preferred_element_type=F32)

    def softmax_pv(c, s_ref, g):
        off = pl.multiple_of(c * bk, bk)
        cols = slice(g * gw, (g + 1) * gw)
        m_old = m_ref[:, cols]
        m_parts, p_parts = [], []
        for j in range(gw // LANES):
            lo = g * gw + j * LANES
            sj = s_ref[:, lo:lo + LANES]
            mj = jnp.maximum(m_old[:, j * LANES:(j + 1) * LANES], jnp.max(sj, axis=0, keepdims=True))
            p_parts.append(jnp.exp2((sj - mj).astype(BF16)))
            m_parts.append(mj)
        m_new = jnp.concatenate(m_parts, axis=1)
        m_ref[:, cols] = m_new
        alpha = jnp.exp2(m_old - m_new)
        rl, rh = v_rows(g)
        pv = jnp.dot(v_ref[rl:rh, pl.ds(off, bk)], jnp.concatenate(p_parts, axis=1),
                     preferred_element_type=F32)
        acc_ref[:, cols] = alpha * acc_ref[:, cols] + pv

    def half(c, cur, nxt, issue_next):
        for g in range(n_groups):
            if issue_next:
                scores(c + 1, nxt, g)
            softmax_pv(c, cur, g)

    def chunks(j, last):
        for u in range(CHUNKS_PER_ITER):
            half(j * CHUNKS_PER_ITER + u, s_refs[u % 2], s_refs[(u + 1) % 2],
                 not (last and u == CHUNKS_PER_ITER - 1))

    acc_ref[...] = jnp.zeros_like(acc_ref)
    m_ref[...] = jnp.full(m_ref.shape, -jnp.inf, F32)
    for g in range(n_groups):
        scores(0, s_refs[0], g)

    def body(j, carry):
        chunks(j, False)
        return carry

    lax.fori_loop(0, n_iters - 1, body, 0)
    chunks(n_iters - 1, True)


def _attn_a_kernel(q_ref, k_ref, v_ref, o_ref, rhs_ref, acc_ref, m_ref, s0, s1, *, seq):
    bq = A_Q_BLOCK
    half_n = A_GROUP * bq
    rhs_ref[...] = jnp.zeros_like(rhs_ref)
    for h in range(A_HEADS):
        kv = h // A_GROUP
        rhs_ref[kv * HEAD_DIM:(kv + 1) * HEAD_DIM, h * bq:(h + 1) * bq] = q_ref[0, h * HEAD_DIM:(h + 1) * HEAD_DIM, :]
    v_rows = lambda g: ((g * COL_GROUP // half_n) * VA_ROWS, (g * COL_GROUP // half_n + 1) * VA_ROWS)
    _column_softmax_attention(k_ref.at[0], rhs_ref, v_ref.at[0], v_rows, acc_ref, m_ref, (s0, s1), seq)
    acc = acc_ref[...]
    o = acc[:HEAD_DIM] * (1.0 / acc[HEAD_DIM:HEAD_DIM + 1])
    for h in range(A_HEADS):
        o_ref[0, h * HEAD_DIM:(h + 1) * HEAD_DIM, :] = o[:, h * bq:(h + 1) * bq].astype(BF16)


def _attention_scratch(acc_rows, n):
    return [pltpu.VMEM((LANES, n), BF16), pltpu.VMEM((acc_rows, n), F32), pltpu.VMEM((1, n), F32),
            pltpu.VMEM((KV_CHUNK, n), F32), pltpu.VMEM((KV_CHUNK, n), F32)]


def _attention_a(qa, ka, va):
    bsz, _, s = qa.shape
    bq = A_Q_BLOCK
    assert s % (CHUNKS_PER_ITER * KV_CHUNK) == 0 and s % bq == 0
    n = A_HEADS * bq
    return pl.pallas_call(
        functools.partial(_attn_a_kernel, seq=s),
        grid=(bsz, s // bq),
        in_specs=[
            pl.BlockSpec((1, A_Q_W, bq), lambda b, i: (b, 0, i)),
            pl.BlockSpec((1, s, LANES), lambda b, i: (b, 0, 0)),
            pl.BlockSpec((1, A_KV_HEADS * VA_ROWS, s), lambda b, i: (b, 0, 0)),
        ],
        out_specs=pl.BlockSpec((1, A_Q_W, bq), lambda b, i: (b, 0, i)),
        out_shape=jax.ShapeDtypeStruct((bsz, A_Q_W, s), BF16),
        scratch_shapes=_attention_scratch(VA_ROWS, n),
        compiler_params=_cparams(("parallel", "parallel")),
        name="attention_a",
    )(qa, ka, va)


def _attn_b_kernel(lam_ref, q_ref, k_ref, v_ref, sg_ref, o_ref, rhs_ref, acc, m_ref, s0, s1, *, seq):
    bq = B_Q_BLOCK
    rhs_ref[...] = jnp.zeros_like(rhs_ref)
    rhs_ref[:HEAD_DIM, :bq] = q_ref[0, :HEAD_DIM, :]
    rhs_ref[HEAD_DIM:, bq:] = q_ref[0, HEAD_DIM:, :]
    _column_softmax_attention(k_ref.at[0, 0], rhs_ref, v_ref.at[0], lambda g: (0, VB_ROWS), acc, m_ref,
                              (s0, s1), seq)
    a = acc[...]
    o = a[:B_V_DIM] * (1.0 / a[B_V_DIM:B_V_DIM + 1])
    o = o[:, :bq] - lam_ref[0] * o[:, bq:]
    o = o * lax.rsqrt(jnp.mean(o * o, axis=0, keepdims=True) + EPS) * sg_ref[...]
    o_ref[0] = (o * (1.0 - LAMBDA_INIT)).astype(BF16)


def _attention_b(lam, qb, kb, vb, subln):
    bsz, _, s = qb.shape
    bq = B_Q_BLOCK
    assert s % (CHUNKS_PER_ITER * KV_CHUNK) == 0 and s % bq == 0
    n = 2 * bq
    return pl.pallas_call(
        functools.partial(_attn_b_kernel, seq=s),
        grid=(bsz, B_HEADS, s // bq),
        in_specs=[
            pl.BlockSpec(memory_space=pltpu.SMEM),
            pl.BlockSpec((1, B_V_DIM, bq), lambda b, h, i: (b, h, i)),
            pl.BlockSpec((1, 1, s, LANES), lambda b, h, i: (b, h, 0, 0)),
            pl.BlockSpec((1, VB_ROWS, s), lambda b, h, i: (b, h, 0)),
            pl.BlockSpec((B_V_DIM, bq), lambda b, h, i: (0, 0)),
        ],
        out_specs=pl.BlockSpec((1, B_V_DIM, bq), lambda b, h, i: (b, h, i)),
        out_shape=jax.ShapeDtypeStruct((bsz, B_V_W, s), BF16),
        scratch_shapes=_attention_scratch(VB_ROWS, n),
        compiler_params=_cparams(("parallel", "parallel", "parallel")),
        name="attention_b",
    )(lam, qb, kb, vb, subln)


def _store_token_tiles(ref, val):
    n = val.shape[0]
    for j in range(TILE_SUBLANES):
        ref[pl.ds(j, n, stride=TILE_SUBLANES), :] = val[:, j * LANES:(j + 1) * LANES]


def _load_token_tiles(ref, n):
    return jnp.concatenate([ref[pl.ds(j, n, stride=TILE_SUBLANES), :] for j in range(TILE_SUBLANES)], axis=1)


def _outproj_kernel(a_ref, b_ref, x_ref, wa_ref, wb_ref, g_ref, rw_ref, rb_ref,
                    x2_ref, xn_ref, te_ref, tg_ref):
    tn = (((0,), (0,)), ((), ()))
    y = (x_ref[0]
         + lax.dot_general(a_ref[0], wa_ref[...], tn, preferred_element_type=F32)
         + lax.dot_general(b_ref[0], wb_ref[...], tn, preferred_element_type=F32))
    x2_ref[0] = y
    xn = y * lax.rsqrt(jnp.mean(y * y, axis=-1, keepdims=True) + EPS) * g_ref[...]
    _store_token_tiles(xn_ref, xn)
    logits = lax.dot_general(rw_ref[...], xn, (((1,), (1,)), ((), ())),
                             precision=lax.Precision.HIGHEST, preferred_element_type=F32) + rb_ref[...]
    iota = lax.broadcasted_iota(jnp.int32, logits.shape, 0)
    work = logits
    vals, idxs = [], []
    for _ in range(TOP_K):
        mx = jnp.max(work, axis=0, keepdims=True)
        idx = jnp.min(jnp.where(work == mx, iota, N_EXPERTS), axis=0, keepdims=True)
        vals.append(mx)
        idxs.append(idx)
        work = jnp.where(iota == idx, -jnp.inf, work)
    ex = [jnp.exp(v - vals[0]) for v in vals]
    inv = 1.0 / (ex[0] + ex[1] + ex[2] + ex[3])
    te_ref[0] = jnp.concatenate(idxs, axis=0)
    tg_ref[0] = jnp.concatenate([e * inv for e in ex], axis=0)


def _out_projection(oa, ob, x, wa, wb, g_ffn, rw_t, rb):
    bsz, s, d = x.shape
    tm = TOKEN_TILE
    const = lambda shape: pl.BlockSpec(shape, lambda b, i: (0,) * len(shape))
    tok = pl.BlockSpec((1, tm, d), lambda b, i: (b, i, 0))
    sel = pl.BlockSpec((1, TOP_K, tm), lambda b, i: (b, 0, i))
    return pl.pallas_call(
        _outproj_kernel,
        grid=(bsz, s // tm),
        in_specs=[
            pl.BlockSpec((1, A_Q_W, tm), lambda b, i: (b, 0, i)),
            pl.BlockSpec((1, B_V_W, tm), lambda b, i: (b, 0, i)),
            tok,
            const((A_Q_W, d)),
            const((B_V_W, d)),
            const((1, d)),
            const((N_EXPERTS, d)),
            const((N_EXPERTS, tm)),
        ],
        out_specs=[tok,
                   pl.BlockSpec((tm * TILE_SUBLANES, LANES), lambda b, i: (b * (s // tm) + i, 0)),
                   sel, sel],
        out_shape=[
            jax.ShapeDtypeStruct((bsz, s, d), F32),
            jax.ShapeDtypeStruct((bsz * s * TILE_SUBLANES, LANES), F32),
            jax.ShapeDtypeStruct((bsz, TOP_K, s), jnp.int32),
            jax.ShapeDtypeStruct((bsz, TOP_K, s), F32),
        ],
        compiler_params=_cparams(("parallel", "parallel")),
        name="out_projection_router",
    )(oa, ob, x, wa, wb, g_ffn, rw_t, rb)


def _tile_rows(r):
    return pl.ds(r * TILE_SUBLANES, TILE_SUBLANES)


def _gather_row(idx_ref, r, src_hbm, buf, sem):
    src = pl.ds(pl.multiple_of(idx_ref[0, 0, r], TILE_SUBLANES), TILE_SUBLANES)
    return pltpu.make_async_copy(src_hbm.at[src, :], buf.at[_tile_rows(r), :], sem)


def _scatter_row(idx_ref, r, buf, dst_hbm, sem):
    dst = pl.ds(pl.multiple_of(idx_ref[0, 0, r], TILE_SUBLANES), TILE_SUBLANES)
    return pltpu.make_async_copy(buf.at[_tile_rows(r), :], dst_hbm.at[dst, :], sem)


def _moe_kernel(be_ref, nb_ref, tok_first, tok_nxt, dst_prev, xn_hbm, wg_ref, bg_ref, wu_ref, bu_ref,
                wd_ref, bd_ref, y_hbm, xbuf_all, ybuf_all, gsem, ssem, *, pad_base):
    i = pl.program_id(0)
    n_used = nb_ref[0]
    rows = MOE_ROWS
    tile_rows = rows * TILE_SUBLANES
    xbufs = tuple(xbuf_all.at[pl.ds(s * tile_rows, tile_rows), :] for s in range(2))
    ybufs = tuple(ybuf_all.at[pl.ds(s * tile_rows, tile_rows), :] for s in range(2))

    @pl.when(i == 0)
    def _():
        ybuf_all[...] = jnp.zeros_like(ybuf_all)
        fills = [pltpu.make_async_copy(ybufs[0], y_hbm.at[pl.ds(pad_base + j * tile_rows, tile_rows), :],
                                       ssem.at[0]) for j in range(N_EXPERTS)]
        for f in fills:
            f.start()
        for f in fills:
            f.wait()
        for r in range(rows):
            _gather_row(tok_first, r, xn_hbm, xbufs[0], gsem.at[0]).start(priority=r % 2)

    def step(slot):
        xbuf, ybuf = xbufs[slot], ybufs[slot]
        y_prev, sem_prev = ybufs[1 - slot], ssem.at[1 - slot]

        @pl.when(i <= n_used)
        def _():
            pltpu.make_async_copy(xn_hbm.at[pl.ds(0, tile_rows), :], xbuf, gsem.at[slot]).wait()

        @pl.when(jnp.logical_and(i >= 1, i - 2 < n_used))
        def _():
            pltpu.make_async_copy(ybuf, y_hbm.at[pl.ds(0, tile_rows), :], ssem.at[slot]).wait()

        @pl.when(i == n_used)
        def _():
            for r in range(rows):
                _scatter_row(dst_prev, r, y_prev, y_hbm, sem_prev).start(priority=r % 2)

        @pl.when(i < n_used)
        def _():
            per_k = rows // K_SPLIT
            kw = D_MODEL // K_SPLIT
            g = bg_ref[0]
            u = bu_ref[0]
            for q in range(K_SPLIT):
                for r in range(q * per_k, (q + 1) * per_k):
                    _gather_row(tok_nxt, r, xn_hbm, xbufs[1 - slot], gsem.at[1 - slot]).start(priority=r % 2)
                xq = jnp.concatenate([xbuf[pl.ds(j, rows, stride=TILE_SUBLANES), :]
                                      for j in range(q * kw // LANES, (q + 1) * kw // LANES)], axis=1).astype(BF16)
                g = g + jnp.dot(xq, wg_ref[0, q * kw:(q + 1) * kw, :], preferred_element_type=F32)
                u = u + jnp.dot(xq, wu_ref[0, q * kw:(q + 1) * kw, :], preferred_element_type=F32)
            g = jnp.minimum(g, SWIGLU_LIMIT)
            u = jnp.clip(u, -SWIGLU_LIMIT, SWIGLU_LIMIT)
            hb = (g * jax.nn.sigmoid(g * SWIGLU_ALPHA) * (u + 1.0)).astype(BF16)
            per_n = rows // N_SPLIT
            nw = D_MODEL // N_SPLIT
            for q in range(N_SPLIT):
                for r in range(q * per_n, (q + 1) * per_n):
                    _scatter_row(dst_prev, r, y_prev, y_hbm, sem_prev).start(priority=r % 2)
                cols = slice(q * nw, (q + 1) * nw)
                yq = jnp.dot(hb, wd_ref[0, :, cols], preferred_element_type=F32) + bd_ref[0, :, cols]
                for j in range(nw // LANES):
                    ybuf[pl.ds(q * nw // LANES + j, rows, stride=TILE_SUBLANES), :] = yq[:, j * LANES:(j + 1) * LANES]

    for slot in range(2):
        pl.when(lax.rem(i, 2) == slot)(functools.partial(step, slot))


def _moe_experts(block_e, n_used, src_tok, dst_row, n_out_rows, xn, wg, bg, wu, bu, wd, bd):
    d = D_MODEL
    t = xn.shape[0] // TILE_SUBLANES
    bm = MOE_ROWS
    n_blocks = src_tok.shape[0]
    dff = wg.shape[2]
    last = n_blocks - 1
    n_steps = n_blocks + 2
    blk = lambda i: jnp.minimum(i, last)
    wspec = lambda k, n: pl.BlockSpec((1, k, n), lambda i, be, nb: (be[blk(i)], 0, 0))
    idx = lambda fn: pl.BlockSpec((1, 1, bm), lambda i, be, nb: (fn(i), 0, 0), memory_space=pltpu.SMEM)
    grid_spec = pltpu.PrefetchScalarGridSpec(
        num_scalar_prefetch=2,
        grid=(n_steps,),
        in_specs=[
            idx(lambda i: 0), idx(lambda i: blk(i + 1)), idx(lambda i: jnp.minimum(i, n_blocks)),
            pl.BlockSpec(memory_space=pl.ANY),
            wspec(d, dff), wspec(1, dff), wspec(d, dff), wspec(1, dff), wspec(dff, d), wspec(1, d),
        ],
        out_specs=pl.BlockSpec(memory_space=pl.ANY),
        scratch_shapes=[pltpu.VMEM((2 * bm * TILE_SUBLANES, LANES), F32)] * 2
                       + [pltpu.SemaphoreType.DMA((2,)), pltpu.SemaphoreType.DMA((2,))],
    )
    return pl.pallas_call(
        functools.partial(_moe_kernel, pad_base=t * TOP_K * TILE_SUBLANES),
        grid_spec=grid_spec,
        out_shape=jax.ShapeDtypeStruct((n_out_rows * TILE_SUBLANES, LANES), F32),
        compiler_params=_cparams(("arbitrary",)),
        name="moe_experts",
    )(block_e, n_used, src_tok, src_tok, dst_row, xn, wg, bg, wu, bu, wd, bd)


def _combine_kernel(y0_ref, y1_ref, y2_ref, y3_ref, x2_ref, gate_ref, g_ref, o_ref):
    tc = x2_ref.shape[0]
    gates = gate_ref[...]
    y = x2_ref[...]
    for k, y_ref in enumerate((y0_ref, y1_ref, y2_ref, y3_ref)):
        y = y + gates[:, k:k + 1] * _load_token_tiles(y_ref, tc)
    o_ref[...] = y * lax.rsqrt(jnp.mean(y * y, axis=-1, keepdims=True) + EPS) * g_ref[...]


def _combine(y, x2, gates, g_final):
    t, d = x2.shape
    tc = COMBINE_TOKENS
    n = t // tc
    yspec = lambda k: pl.BlockSpec((tc * TILE_SUBLANES, LANES), lambda i: (k * n + i, 0))
    return pl.pallas_call(
        _combine_kernel,
        grid=(n,),
        in_specs=[
            yspec(0), yspec(1), yspec(2), yspec(3),
            pl.BlockSpec((tc, d), lambda i: (i, 0)),
            pl.BlockSpec((tc, TOP_K), lambda i: (i, 0)),
            pl.BlockSpec((1, d), lambda i: (0, 0)),
        ],
        out_specs=pl.BlockSpec((tc, d), lambda i: (i, 0)),
        out_shape=jax.ShapeDtypeStruct((t, d), F32),
        compiler_params=_cparams(("parallel",)),
        name="combine_final_norm",
    )(y, y, y, y, x2, gates, g_final)


def _route(top_e):
    t = top_e.shape[0]
    a = t * TOP_K
    bm = MOE_ROWS
    n_blocks = a // bm + N_EXPERTS
    flat_e = top_e.reshape(-1)
    order = jnp.argsort(flat_e, stable=True).astype(jnp.int32)
    counts = jnp.bincount(flat_e, length=N_EXPERTS).astype(jnp.int32)
    padded = ((counts + bm - 1) // bm) * bm
    start = jnp.cumsum(counts) - counts
    pend = jnp.cumsum(padded)
    pstart = pend - padded
    block_start = jnp.arange(n_blocks, dtype=jnp.int32) * bm
    block_e = jnp.minimum(jnp.sum((pend[None, :] <= block_start[:, None]).astype(jnp.int32), axis=1),
                          N_EXPERTS - 1)
    rows = jnp.arange(n_blocks * bm, dtype=jnp.int32)
    row_e = jnp.repeat(block_e, bm)
    within = rows - pstart[row_e]
    valid = within < counts[row_e]
    assign = order[jnp.clip(start[row_e] + within, 0, a - 1)]
    pad_before = jnp.cumsum(padded - counts) - (padded - counts)
    pad_rank = jnp.minimum(pad_before[row_e] + within - counts[row_e], N_EXPERTS * bm - 1)
    tok = assign // TOP_K
    src = (jnp.where(valid, tok, 0) * TILE_SUBLANES).astype(jnp.int32)
    dst = (jnp.where(valid, (assign % TOP_K) * t + tok, a + pad_rank) * TILE_SUBLANES).astype(jnp.int32)
    dummy = (a + jnp.arange(bm, dtype=jnp.int32)) * TILE_SUBLANES
    dst = jnp.concatenate([dummy, dst]).reshape(n_blocks + 1, 1, bm)
    n_used = (pend[-1] // bm).astype(jnp.int32).reshape(1)
    return block_e, n_used, src.reshape(n_blocks, 1, bm), dst, a + N_EXPERTS * bm


def _rope_tables(seq):
    rows = seq // GRID_W
    row = jnp.repeat(jnp.arange(rows, dtype=F32), GRID_W)
    col = jnp.tile(jnp.arange(GRID_W, dtype=F32), rows)
    half = HEAD_DIM // 2
    inv = AXIAL_THETA ** (-jnp.arange(0, half, 2, dtype=F32) / half)
    ang = jnp.concatenate([row[:, None] * inv, col[:, None] * inv], axis=-1)
    ang = jnp.concatenate([ang, ang], axis=-1).T
    sign_a = jnp.where(jnp.arange(HEAD_DIM) < half, -1.0, 1.0).astype(F32)[:, None]
    tt = jnp.arange(seq, dtype=F32)
    invb = ROPE_THETA ** (-jnp.arange(0, PARTIAL_ROT, 2, dtype=F32) / PARTIAL_ROT)
    angb = tt[:, None] * invb
    angb = jnp.concatenate([angb, angb], axis=-1).T
    sign_b = jnp.where(jnp.arange(PARTIAL_ROT) < PARTIAL_ROT // 2, -1.0, 1.0).astype(F32)[:, None]
    return jnp.cos(ang), jnp.sin(ang) * sign_a, jnp.cos(angb), jnp.sin(angb) * sign_b


def _trunk(x, p):
    bsz, s, d = x.shape
    t = bsz * s
    qa, qb, va, vb, ka, kb = _in_projection(x, p["g_attn"], p["w_in_t"], _rope_tables(s), p["qg"], p["kg"])
    oa = _attention_a(qa, ka, va)
    ob = _attention_b(p["lam"], qb, kb, vb, p["subln"])
    x2, xn, te, tg = _out_projection(oa, ob, x, p["wa"], p["wb"], p["g_ffn"], p["rw_t"], p["rb"])
    top_e = jnp.swapaxes(te, 1, 2).reshape(t, TOP_K)
    gates = jnp.swapaxes(tg, 1, 2).reshape(t, TOP_K)
    block_e, n_used, src_tok, dst_row, n_out_rows = _route(top_e)
    y = _moe_experts(block_e, n_used, src_tok, dst_row, n_out_rows, xn,
                     p["wg"], p["bg"], p["wu"], p["bu"], p["wd"], p["bd"])
    out = _combine(y, x2.reshape(t, d), gates, p["g_final"])
    return out.reshape(bsz, s, d)


def kernel(x_prompt, x_sample, attn_norm, w_in, a_q_norm, a_k_norm, b_lambda_q1, b_lambda_k1, b_lambda_q2,
           b_lambda_k2, b_subln, w_out, ffn_norm, router_w, router_b, w_gate, b_gate, w_up, b_up, w_down,
           b_down, final_norm):
    tm = TOKEN_TILE
    lam = (jnp.exp(jnp.sum(b_lambda_q1[0].astype(F32) * b_lambda_k1[0].astype(F32)))
           - jnp.exp(jnp.sum(b_lambda_q2[0].astype(F32) * b_lambda_k2[0].astype(F32)))
           + LAMBDA_INIT)
    p = {
        "g_attn": attn_norm[0].reshape(1, D_MODEL),
        "w_in_t": w_in[0].T.astype(BF16),
        "qg": jnp.broadcast_to(a_q_norm[0][:, None], (HEAD_DIM, tm)),
        "kg": jnp.broadcast_to(a_k_norm[0][:, None], (HEAD_DIM, tm)),
        "lam": lam.reshape(1).astype(F32),
        "subln": jnp.broadcast_to(b_subln[0][:, None], (B_V_DIM, B_Q_BLOCK)),
        "wa": w_out[0, :A_Q_W].astype(BF16),
        "wb": w_out[0, A_Q_W:].astype(BF16),
        "g_ffn": ffn_norm[0].reshape(1, D_MODEL),
        "rw_t": router_w[0].T,
        "rb": jnp.broadcast_to(router_b[0][:, None], (N_EXPERTS, tm)),
        "wg": w_gate[0].astype(BF16),
        "bg": b_gate[0].reshape(N_EXPERTS, 1, -1),
        "wu": w_up[0].astype(BF16),
        "bu": b_up[0].reshape(N_EXPERTS, 1, -1),
        "wd": w_down[0].astype(BF16),
        "bd": b_down[0].reshape(N_EXPERTS, 1, -1),
        "g_final": final_norm.reshape(1, D_MODEL),
    }
    return _trunk(x_prompt, p), _trunk(x_sample, p)
```

```python
import functools
import math

import jax
import jax.numpy as jnp
from jax import lax
from jax.experimental import pallas as pl
from jax.experimental.pallas import tpu as pltpu

F32 = jnp.float32
BF16 = jnp.bfloat16

D_MODEL = 1024
HEAD_DIM = 64
A_HEADS = 8
A_KV_HEADS = 2
A_GROUP = A_HEADS // A_KV_HEADS
B_HEADS = 4
B_V_DIM = 2 * HEAD_DIM
A_Q_W = A_HEADS * HEAD_DIM
A_KV_W = A_KV_HEADS * HEAD_DIM
B_QK_W = B_HEADS * 2 * HEAD_DIM
B_V_W = B_HEADS * B_V_DIM
IN_W = A_Q_W + 2 * A_KV_W + 2 * B_QK_W + B_V_W
GRID_W = 64
AXIAL_THETA = 10000.0
ROPE_THETA = 500000.0
PARTIAL_ROT = HEAD_DIM // 4
N_EXPERTS = 32
TOP_K = 4
SWIGLU_LIMIT = 7.0
SWIGLU_ALPHA = 1.702
EPS = 1e-6
LAMBDA_INIT = 0.8 - 0.6 * math.exp(-0.3 * 0)
Q_PRESCALE = HEAD_DIM ** -0.5 * math.log2(math.e)

O_AQ = 0
O_AK = O_AQ + A_Q_W
O_AV = O_AK + A_KV_W
O_BQ = O_AV + A_KV_W
O_BK = O_BQ + B_QK_W
O_BV = O_BK + B_QK_W

LANES = 128
TILE_SUBLANES = 8
BF16_SUBLANES = 16
ONES_ROWS = BF16_SUBLANES
VA_ROWS = HEAD_DIM + ONES_ROWS
VB_ROWS = B_V_DIM + ONES_ROWS
TOKEN_TILE = 512
A_Q_BLOCK = 256
B_Q_BLOCK = 1024
KV_CHUNK = 256
CHUNKS_PER_ITER = 8
COL_GROUP = 256
MOE_ROWS = 256
K_SPLIT = 4
N_SPLIT = 4
COMBINE_TOKENS = 256
VMEM_LIMIT = 48 * 1024 * 1024


def _cparams(sem):
    return pltpu.CompilerParams(dimension_semantics=sem, vmem_limit_bytes=VMEM_LIMIT)


def _inproj_kernel(x_ref, g_ref, w_ref, cosa_ref, sina_ref, cosb_ref, sinb_ref, qg_ref, kg_ref,
                   qa_ref, qb_ref, va_ref, vb_ref, ka_ref, kb_ref):
    x = x_ref[0]
    tm = x.shape[0]
    ms = jnp.mean(x * x, axis=-1, keepdims=True)
    n = (x * lax.rsqrt(ms + EPS) * g_ref[...]).astype(BF16)
    pt = lax.dot_general(w_ref[...], n, (((1,), (1,)), ((), ())), preferred_element_type=F32)

    cosa = cosa_ref[...]
    sina = sina_ref[...]
    cosb = cosb_ref[...]
    sinb = sinb_ref[...]
    half = HEAD_DIM // 2
    hrot = PARTIAL_ROT // 2

    def norm_rope_a(t, gain):
        y = t * lax.rsqrt(jnp.mean(t * t, axis=0, keepdims=True) + EPS) * gain
        rot = jnp.concatenate([y[half:], y[:half]], axis=0)
        return y * cosa + rot * sina

    def rope_b(t):
        head = t[:PARTIAL_ROT]
        rot = jnp.concatenate([head[hrot:], head[:hrot]], axis=0)
        return jnp.concatenate([head * cosb + rot * sinb, t[PARTIAL_ROT:]], axis=0)

    qg = qg_ref[...]
    kg = kg_ref[...]
    ones = jnp.ones((ONES_ROWS, tm), BF16)
    for h in range(A_HEADS):
        t = pt[O_AQ + h * HEAD_DIM:O_AQ + (h + 1) * HEAD_DIM]
        qa_ref[0, h * HEAD_DIM:(h + 1) * HEAD_DIM, :] = (norm_rope_a(t, qg) * Q_PRESCALE).astype(BF16)
    for h in range(A_KV_HEADS):
        va_ref[0, h * VA_ROWS:h * VA_ROWS + HEAD_DIM, :] = (
            pt[O_AV + h * HEAD_DIM:O_AV + (h + 1) * HEAD_DIM].astype(BF16))
        va_ref[0, h * VA_ROWS + HEAD_DIM:(h + 1) * VA_ROWS, :] = ones
    for j in range(B_QK_W // HEAD_DIM):
        t = pt[O_BQ + j * HEAD_DIM:O_BQ + (j + 1) * HEAD_DIM]
        qb_ref[0, j * HEAD_DIM:(j + 1) * HEAD_DIM, :] = (rope_b(t) * Q_PRESCALE).astype(BF16)
    for h in range(B_HEADS):
        vb_ref[0, h * VB_ROWS:h * VB_ROWS + B_V_DIM, :] = (
            pt[O_BV + h * B_V_DIM:O_BV + (h + 1) * B_V_DIM].astype(BF16))
        vb_ref[0, h * VB_ROWS + B_V_DIM:(h + 1) * VB_ROWS, :] = ones

    ka = jnp.concatenate([norm_rope_a(pt[O_AK + h * HEAD_DIM:O_AK + (h + 1) * HEAD_DIM], kg)
                          for h in range(A_KV_HEADS)], axis=0)
    ka_ref[0] = ka.T.astype(BF16)
    for h in range(B_HEADS):
        kb = jnp.concatenate([rope_b(pt[O_BK + (2 * h + c) * HEAD_DIM:O_BK + (2 * h + c + 1) * HEAD_DIM])
                              for c in range(2)], axis=0)
        kb_ref[0, h] = kb.T.astype(BF16)


def _in_projection(x, g_attn, w_in_t, tabs, qg, kg):
    bsz, s, d = x.shape
    tm = TOKEN_TILE
    cosa, sina, cosb, sinb = tabs
    const = lambda shape: pl.BlockSpec(shape, lambda b, i: (0,) * len(shape))
    rows = lambda r: pl.BlockSpec((1, r, tm), lambda b, i: (b, 0, i))
    return pl.pallas_call(
        _inproj_kernel,
        grid=(bsz, s // tm),
        in_specs=[
            pl.BlockSpec((1, tm, d), lambda b, i: (b, i, 0)),
            const((1, d)),
            const((IN_W, d)),
            pl.BlockSpec((HEAD_DIM, tm), lambda b, i: (0, i)),
            pl.BlockSpec((HEAD_DIM, tm), lambda b, i: (0, i)),
            pl.BlockSpec((PARTIAL_ROT, tm), lambda b, i: (0, i)),
            pl.BlockSpec((PARTIAL_ROT, tm), lambda b, i: (0, i)),
            const((HEAD_DIM, tm)),
            const((HEAD_DIM, tm)),
        ],
        out_specs=[
            rows(A_Q_W), rows(B_QK_W), rows(A_KV_HEADS * VA_ROWS), rows(B_HEADS * VB_ROWS),
            pl.BlockSpec((1, tm, LANES), lambda b, i: (b, i, 0)),
            pl.BlockSpec((1, B_HEADS, tm, LANES), lambda b, i: (b, 0, i, 0)),
        ],
        out_shape=[
            jax.ShapeDtypeStruct((bsz, A_Q_W, s), BF16),
            jax.ShapeDtypeStruct((bsz, B_QK_W, s), BF16),
            jax.ShapeDtypeStruct((bsz, A_KV_HEADS * VA_ROWS, s), BF16),
            jax.ShapeDtypeStruct((bsz, B_HEADS * VB_ROWS, s), BF16),
            jax.ShapeDtypeStruct((bsz, s, LANES), BF16),
            jax.ShapeDtypeStruct((bsz, B_HEADS, s, LANES), BF16),
        ],
        compiler_params=_cparams(("parallel", "parallel")),
        name="in_projection",
    )(x, g_attn, w_in_t, cosa, sina, cosb, sinb, qg, kg)


def _column_softmax_attention(k_ref, rhs_ref, v_ref, v_rows, acc_ref, m_ref, s_refs, seq):
    n = rhs_ref.shape[1]
    bk = KV_CHUNK
    gw = COL_GROUP
    n_groups = n // gw
    n_iters = seq // (CHUNKS_PER_ITER * bk)

    def scores(c, s_ref, g):
        off = pl.multiple_of(c * bk, bk)
        s_ref[:, g * gw:(g + 1) * gw] = jnp.dot(k_ref[pl.ds(off, bk), :], rhs_ref[:, g * gw:(g + 1) * gw],
                                                preferred_element_type=F32)

    def softmax_pv(c, s_ref, g):
        off = pl.multiple_of(c * bk, bk)
        cols = slice(g * gw, (g + 1) * gw)
        m_old = m_ref[:, cols]
        m_parts, p_parts = [], []
        for j in range(gw // LANES):
            lo = g * gw + j * LANES
            sj = s_ref[:, lo:lo + LANES]
            mj = jnp.maximum(m_old[:, j * LANES:(j + 1) * LANES], jnp.max(sj, axis=0, keepdims=True))
            p_parts.append(jnp.exp2((sj - mj).astype(BF16)))
            m_parts.append(mj)
        m_new = jnp.concatenate(m_parts, axis=1)
        m_ref[:, cols] = m_new
        alpha = jnp.exp2(m_old - m_new)
        rl, rh = v_rows(g)
        pv = jnp.dot(v_ref[rl:rh, pl.ds(off, bk)], jnp.concatenate(p_parts, axis=1),
                     preferred_element_type=F32)
        acc_ref[:, cols] = alpha * acc_ref[:, cols] + pv

    def half(c, cur, nxt, issue_next):
        for g in range(n_groups):
            if issue_next:
                scores(c + 1, nxt, g)
            softmax_pv(c, cur, g)

    def chunks(j, last):
        for u in range(CHUNKS_PER_ITER):
            half(j * CHUNKS_PER_ITER + u, s_refs[u % 2], s_refs[(u + 1) % 2],
                 not (last and u == CHUNKS_PER_ITER - 1))

    acc_ref[...] = jnp.zeros_like(acc_ref)
    m_ref[...] = jnp.full(m_ref.shape, -jnp.inf, F32)
    for g in range(n_groups):
        scores(0, s_refs[0], g)

    def body(j, carry):
        chunks(j, False)
        return carry

    lax.fori_loop(0, n_iters - 1, body, 0)
    chunks(n_iters - 1, True)


def _attn_a_kernel(q_ref, k_ref, v_ref, o_ref, rhs_ref, acc_ref, m_ref, s0, s1, *, seq):
    bq = A_Q_BLOCK
    half_n = A_GROUP * bq
    rhs_ref[...] = jnp.zeros_like(rhs_ref)
    for h in range(A_HEADS):
        kv = h // A_GROUP
        rhs_ref[kv * HEAD_DIM:(kv + 1) * HEAD_DIM, h * bq:(h + 1) * bq] = q_ref[0, h * HEAD_DIM:(h + 1) * HEAD_DIM, :]
    v_rows = lambda g: ((g * COL_GROUP // half_n) * VA_ROWS, (g * COL_GROUP // half_n + 1) * VA_ROWS)
    _column_softmax_attention(k_ref.at[0], rhs_ref, v_ref.at[0], v_rows, acc_ref, m_ref, (s0, s1), seq)
    acc = acc_ref[...]
    o = acc[:HEAD_DIM] * (1.0 / acc[HEAD_DIM:HEAD_DIM + 1])
    for h in range(A_HEADS):
        o_ref[0, h * HEAD_DIM:(h + 1) * HEAD_DIM, :] = o[:, h * bq:(h + 1) * bq].astype(BF16)


def _attention_scratch(acc_rows, n):
    return [pltpu.VMEM((LANES, n), BF16), pltpu.VMEM((acc_rows, n), F32), pltpu.VMEM((1, n), F32),
            pltpu.VMEM((KV_CHUNK, n), F32), pltpu.VMEM((KV_CHUNK, n), F32)]


def _attention_a(qa, ka, va):
    bsz, _, s = qa.shape
    bq = A_Q_BLOCK
    assert s % (CHUNKS_PER_ITER * KV_CHUNK) == 0 and s % bq == 0
    n = A_HEADS * bq
    return pl.pallas_call(
        functools.partial(_attn_a_kernel, seq=s),
        grid=(bsz, s // bq),
        in_specs=[
            pl.BlockSpec((1, A_Q_W, bq), lambda b, i: (b, 0, i)),
            pl.BlockSpec((1, s, LANES), lambda b, i: (b, 0, 0)),
            pl.BlockSpec((1, A_KV_HEADS * VA_ROWS, s), lambda b, i: (b, 0, 0)),
        ],
        out_specs=pl.BlockSpec((1, A_Q_W, bq), lambda b, i: (b, 0, i)),
        out_shape=jax.ShapeDtypeStruct((bsz, A_Q_W, s), BF16),
        scratch_shapes=_attention_scratch(VA_ROWS, n),
        compiler_params=_cparams(("parallel", "parallel")),
        name="attention_a",
    )(qa, ka, va)


def _attn_b_kernel(lam_ref, q_ref, k_ref, v_ref, sg_ref, o_ref, rhs_ref, acc, m_ref, s0, s1, *, seq):
    bq = B_Q_BLOCK
    rhs_ref[...] = jnp.zeros_like(rhs_ref)
    rhs_ref[:HEAD_DIM, :bq] = q_ref[0, :HEAD_DIM, :]
    rhs_ref[HEAD_DIM:, bq:] = q_ref[0, HEAD_DIM:, :]
    _column_softmax_attention(k_ref.at[0, 0], rhs_ref, v_ref.at[0], lambda g: (0, VB_ROWS), acc, m_ref,
                              (s0, s1), seq)
    a = acc[...]
    o = a[:B_V_DIM] * (1.0 / a[B_V_DIM:B_V_DIM + 1])
    o = o[:, :bq] - lam_ref[0] * o[:, bq:]
    o = o * lax.rsqrt(jnp.mean(o * o, axis=0, keepdims=True) + EPS) * sg_ref[...]
    o_ref[0] = (o * (1.0 - LAMBDA_INIT)).astype(BF16)


def _attention_b(lam, qb, kb, vb, subln):
    bsz, _, s = qb.shape
    bq = B_Q_BLOCK
    assert s % (CHUNKS_PER_ITER * KV_CHUNK) == 0 and s % bq == 0
    n = 2 * bq
    return pl.pallas_call(
        functools.partial(_attn_b_kernel, seq=s),
        grid=(bsz, B_HEADS, s // bq),
        in_specs=[
            pl.BlockSpec(memory_space=pltpu.SMEM),
            pl.BlockSpec((1, B_V_DIM, bq), lambda b, h, i: (b, h, i)),
            pl.BlockSpec((1, 1, s, LANES), lambda b, h, i: (b, h, 0, 0)),
            pl.BlockSpec((1, VB_ROWS, s), lambda b, h, i: (b, h, 0)),
            pl.BlockSpec((B_V_DIM, bq), lambda b, h, i: (0, 0)),
        ],
        out_specs=pl.BlockSpec((1, B_V_DIM, bq), lambda b, h, i: (b, h, i)),
        out_shape=jax.ShapeDtypeStruct((bsz, B_V_W, s), BF16),
        scratch_shapes=_attention_scratch(VB_ROWS, n),
        compiler_params=_cparams(("parallel", "parallel", "parallel")),
        name="attention_b",
    )(lam, qb, kb, vb, subln)


def _store_token_tiles(ref, val):
    n = val.shape[0]
    for j in range(TILE_SUBLANES):
        ref[pl.ds(j, n, stride=TILE_SUBLANES), :] = val[:, j * LANES:(j + 1) * LANES]


def _load_token_tiles(ref, n):
    return jnp.concatenate([ref[pl.ds(j, n, stride=TILE_SUBLANES), :] for j in range(TILE_SUBLANES)], axis=1)


def _outproj_kernel(a_ref, b_ref, x_ref, wa_ref, wb_ref, g_ref, rw_ref, rb_ref,
                    x2_ref, xn_ref, te_ref, tg_ref):
    tn = (((0,), (0,)), ((), ()))
    y = (x_ref[0]
         + lax.dot_general(a_ref[0], wa_ref[...], tn, preferred_element_type=F32)
         + lax.dot_general(b_ref[0], wb_ref[...], tn, preferred_element_type=F32))
    x2_ref[0] = y
    xn = y * lax.rsqrt(jnp.mean(y * y, axis=-1, keepdims=True) + EPS) * g_ref[...]
    _store_token_tiles(xn_ref, xn)
    logits = lax.dot_general(rw_ref[...], xn, (((1,), (1,)), ((), ())),
                             precision=lax.Precision.HIGHEST, preferred_element_type=F32) + rb_ref[...]
    iota = lax.broadcasted_iota(jnp.int32, logits.shape, 0)
    work = logits
    vals, idxs = [], []
    for _ in range(TOP_K):
        mx = jnp.max(work, axis=0, keepdims=True)
        idx = jnp.min(jnp.where(work == mx, iota, N_EXPERTS), axis=0, keepdims=True)
        vals.append(mx)
        idxs.append(idx)
        work = jnp.where(iota == idx, -jnp.inf, work)
    ex = [jnp.exp(v - vals[0]) for v in vals]
    inv = 1.0 / (ex[0] + ex[1] + ex[2] + ex[3])
    te_ref[0] = jnp.concatenate(idxs, axis=0)
    tg_ref[0] = jnp.concatenate([e * inv for e in ex], axis=0)


def _out_projection(oa, ob, x, wa, wb, g_ffn, rw_t, rb):
    bsz, s, d = x.shape
    tm = TOKEN_TILE
    const = lambda shape: pl.BlockSpec(shape, lambda b, i: (0,) * len(shape))
    tok = pl.BlockSpec((1, tm, d), lambda b, i: (b, i, 0))
    sel = pl.BlockSpec((1, TOP_K, tm), lambda b, i: (b, 0, i))
    return pl.pallas_call(
        _outproj_kernel,
        grid=(bsz, s // tm),
        in_specs=[
            pl.BlockSpec((1, A_Q_W, tm), lambda b, i: (b, 0, i)),
            pl.BlockSpec((1, B_V_W, tm), lambda b, i: (b, 0, i)),
            tok,
            const((A_Q_W, d)),
            const((B_V_W, d)),
            const((1, d)),
            const((N_EXPERTS, d)),
            const((N_EXPERTS, tm)),
        ],
        out_specs=[tok,
                   pl.BlockSpec((tm * TILE_SUBLANES, LANES), lambda b, i: (b * (s // tm) + i, 0)),
                   sel, sel],
        out_shape=[
            jax.ShapeDtypeStruct((bsz, s, d), F32),
            jax.ShapeDtypeStruct((bsz * s * TILE_SUBLANES, LANES), F32),
            jax.ShapeDtypeStruct((bsz, TOP_K, s), jnp.int32),
            jax.ShapeDtypeStruct((bsz, TOP_K, s), F32),
        ],
        compiler_params=_cparams(("parallel", "parallel")),
        name="out_projection_router",
    )(oa, ob, x, wa, wb, g_ffn, rw_t, rb)


def _tile_rows(r):
    return pl.ds(r * TILE_SUBLANES, TILE_SUBLANES)


def _gather_row(idx_ref, r, src_hbm, buf, sem):
    src = pl.ds(pl.multiple_of(idx_ref[0, 0, r], TILE_SUBLANES), TILE_SUBLANES)
    return pltpu.make_async_copy(src_hbm.at[src, :], buf.at[_tile_rows(r), :], sem)


def _scatter_row(idx_ref, r, buf, dst_hbm, sem):
    dst = pl.ds(pl.multiple_of(idx_ref[0, 0, r], TILE_SUBLANES), TILE_SUBLANES)
    return pltpu.make_async_copy(buf.at[_tile_rows(r), :], dst_hbm.at[dst, :], sem)


def _moe_kernel(be_ref, nb_ref, tok_first, tok_nxt, dst_prev, xn_hbm, wg_ref, bg_ref, wu_ref, bu_ref,
                wd_ref, bd_ref, y_hbm, xbuf_all, ybuf_all, gsem, ssem, *, pad_base):
    i = pl.program_id(0)
    n_used = nb_ref[0]
    rows = MOE_ROWS
    tile_rows = rows * TILE_SUBLANES
    xbufs = tuple(xbuf_all.at[pl.ds(s * tile_rows, tile_rows), :] for s in range(2))
    ybufs = tuple(ybuf_all.at[pl.ds(s * tile_rows, tile_rows), :] for s in range(2))

    @pl.when(i == 0)
    def _():
        ybuf_all[...] = jnp.zeros_like(ybuf_all)
        fills = [pltpu.make_async_copy(ybufs[0], y_hbm.at[pl.ds(pad_base + j * tile_rows, tile_rows), :],
                                       ssem.at[0]) for j in range(N_EXPERTS)]
        for f in fills:
            f.start()
        for f in fills:
            f.wait()
        fills[1].start()
        for r in range(rows):
            _gather_row(tok_first, r, xn_hbm, xbufs[0], gsem.at[0]).start(priority=r % 2)

    def step(slot):
        xbuf, ybuf = xbufs[slot], ybufs[slot]
        y_prev, sem_prev = ybufs[1 - slot], ssem.at[1 - slot]

        @pl.when(i <= n_used)
        def _():
            pltpu.make_async_copy(xn_hbm.at[pl.ds(0, tile_rows), :], xbuf, gsem.at[slot]).wait()

        def wait_scatter():
            pltpu.make_async_copy(ybuf, y_hbm.at[pl.ds(0, tile_rows), :], ssem.at[slot]).wait()

        pl.when(jnp.logical_and(i >= n_used, i - 2 < n_used))(wait_scatter)

        @pl.when(i == n_used)
        def _():
            for r in range(rows):
                _scatter_row(dst_prev, r, y_prev, y_hbm, sem_prev).start(priority=r % 2)

        @pl.when(i < n_used)
        def _():
            per_k = rows // K_SPLIT
            kw = D_MODEL // K_SPLIT
            g = bg_ref[0]
            u = bu_ref[0]
            for q in range(K_SPLIT):
                for r in range(q * per_k, (q + 1) * per_k):
                    _gather_row(tok_nxt, r, xn_hbm, xbufs[1 - slot], gsem.at[1 - slot]).start(priority=r % 2)
                xq = jnp.concatenate([xbuf[pl.ds(j, rows, stride=TILE_SUBLANES), :]
                                      for j in range(q * kw // LANES, (q + 1) * kw // LANES)], axis=1).astype(BF16)
                g = g + jnp.dot(xq, wg_ref[0, q * kw:(q + 1) * kw, :], preferred_element_type=F32)
                u = u + jnp.dot(xq, wu_ref[0, q * kw:(q + 1) * kw, :], preferred_element_type=F32)
            g = jnp.minimum(g, SWIGLU_LIMIT)
            u = jnp.clip(u, -SWIGLU_LIMIT, SWIGLU_LIMIT)
            hb = (g * jax.nn.sigmoid(g * SWIGLU_ALPHA) * (u + 1.0)).astype(BF16)
            wait_scatter()
            per_n = rows // N_SPLIT
            nw = D_MODEL // N_SPLIT
            for q in range(N_SPLIT):
                for r in range(q * per_n, (q + 1) * per_n):
                    _scatter_row(dst_prev, r, y_prev, y_hbm, sem_prev).start(priority=r % 2)
                cols = slice(q * nw, (q + 1) * nw)
                yq = jnp.dot(hb, wd_ref[0, :, cols], preferred_element_type=F32) + bd_ref[0, :, cols]
                for j in range(nw // LANES):
                    ybuf[pl.ds(q * nw // LANES + j, rows, stride=TILE_SUBLANES), :] = yq[:, j * LANES:(j + 1) * LANES]

    for slot in range(2):
        pl.when(lax.rem(i, 2) == slot)(functools.partial(step, slot))


def _moe_experts(block_e, n_used, src_tok, dst_row, n_out_rows, xn, wg, bg, wu, bu, wd, bd):
    d = D_MODEL
    t = xn.shape[0] // TILE_SUBLANES
    bm = MOE_ROWS
    n_blocks = src_tok.shape[0]
    dff = wg.shape[2]
    last = n_blocks - 1
    n_steps = n_blocks + 2
    blk = lambda i: jnp.minimum(i, last)
    wspec = lambda k, n: pl.BlockSpec((1, k, n), lambda i, be, nb: (be[blk(i)], 0, 0))
    idx = lambda fn: pl.BlockSpec((1, 1, bm), lambda i, be, nb: (fn(i), 0, 0), memory_space=pltpu.SMEM)
    grid_spec = pltpu.PrefetchScalarGridSpec(
        num_scalar_prefetch=2,
        grid=(n_steps,),
        in_specs=[
            idx(lambda i: 0), idx(lambda i: blk(i + 1)), idx(lambda i: jnp.minimum(i, n_blocks)),
            pl.BlockSpec(memory_space=pl.ANY),
            wspec(d, dff), wspec(1, dff), wspec(d, dff), wspec(1, dff), wspec(dff, d), wspec(1, d),
        ],
        out_specs=pl.BlockSpec(memory_space=pl.ANY),
        scratch_shapes=[pltpu.VMEM((2 * bm * TILE_SUBLANES, LANES), F32)] * 2
                       + [pltpu.SemaphoreType.DMA((2,)), pltpu.SemaphoreType.DMA((2,))],
    )
    return pl.pallas_call(
        functools.partial(_moe_kernel, pad_base=t * TOP_K * TILE_SUBLANES),
        grid_spec=grid_spec,
        out_shape=jax.ShapeDtypeStruct((n_out_rows * TILE_SUBLANES, LANES), F32),
        compiler_params=_cparams(("arbitrary",)),
        name="moe_experts",
    )(block_e, n_used, src_tok, src_tok, dst_row, xn, wg, bg, wu, bu, wd, bd)


def _combine_kernel(y0_ref, y1_ref, y2_ref, y3_ref, x2_ref, gate_ref, g_ref, o_ref):
    tc = x2_ref.shape[0]
    gates = gate_ref[...]
    y = x2_ref[...]
    for k, y_ref in enumerate((y0_ref, y1_ref, y2_ref, y3_ref)):
        y = y + gates[:, k:k + 1] * _load_token_tiles(y_ref, tc)
    o_ref[...] = y * lax.rsqrt(jnp.mean(y * y, axis=-1, keepdims=True) + EPS) * g_ref[...]


def _combine(y, x2, gates, g_final):
    t, d = x2.shape
    tc = COMBINE_TOKENS
    n = t // tc
    yspec = lambda k: pl.BlockSpec((tc * TILE_SUBLANES, LANES), lambda i: (k * n + i, 0))
    return pl.pallas_call(
        _combine_kernel,
        grid=(n,),
        in_specs=[
            yspec(0), yspec(1), yspec(2), yspec(3),
            pl.BlockSpec((tc, d), lambda i: (i, 0)),
            pl.BlockSpec((tc, TOP_K), lambda i: (i, 0)),
            pl.BlockSpec((1, d), lambda i: (0, 0)),
        ],
        out_specs=pl.BlockSpec((tc, d), lambda i: (i, 0)),
        out_shape=jax.ShapeDtypeStruct((t, d), F32),
        compiler_params=_cparams(("parallel",)),
        name="combine_final_norm",
    )(y, y, y, y, x2, gates, g_final)


def _route(top_e):
    t = top_e.shape[0]
    a = t * TOP_K
    bm = MOE_ROWS
    n_blocks = a // bm + N_EXPERTS
    flat_e = top_e.reshape(-1)
    order = jnp.argsort(flat_e, stable=True).astype(jnp.int32)
    counts = jnp.bincount(flat_e, length=N_EXPERTS).astype(jnp.int32)
    padded = ((counts + bm - 1) // bm) * bm
    start = jnp.cumsum(counts) - counts
    pend = jnp.cumsum(padded)
    pstart = pend - padded
    block_start = jnp.arange(n_blocks, dtype=jnp.int32) * bm
    block_e = jnp.minimum(jnp.sum((pend[None, :] <= block_start[:, None]).astype(jnp.int32), axis=1),
                          N_EXPERTS - 1)
    rows = jnp.arange(n_blocks * bm, dtype=jnp.int32)
    row_e = jnp.repeat(block_e, bm)
    within = rows - pstart[row_e]
    valid = within < counts[row_e]
    assign = order[jnp.clip(start[row_e] + within, 0, a - 1)]
    pad_before = jnp.cumsum(padded - counts) - (padded - counts)
    pad_rank = jnp.minimum(pad_before[row_e] + within - counts[row_e], N_EXPERTS * bm - 1)
    tok = assign // TOP_K
    src = (jnp.where(valid, tok, 0) * TILE_SUBLANES).astype(jnp.int32)
    dst = (jnp.where(valid, (assign % TOP_K) * t + tok, a + pad_rank) * TILE_SUBLANES).astype(jnp.int32)
    dummy = (a + jnp.arange(bm, dtype=jnp.int32)) * TILE_SUBLANES
    dst = jnp.concatenate([dummy, dst]).reshape(n_blocks + 1, 1, bm)
    n_used = (pend[-1] // bm).astype(jnp.int32).reshape(1)
    return block_e, n_used, src.reshape(n_blocks, 1, bm), dst, a + N_EXPERTS * bm


def _rope_tables(seq):
    rows = seq // GRID_W
    row = jnp.repeat(jnp.arange(rows, dtype=F32), GRID_W)
    col = jnp.tile(jnp.arange(GRID_W, dtype=F32), rows)
    half = HEAD_DIM // 2
    inv = AXIAL_THETA ** (-jnp.arange(0, half, 2, dtype=F32) / half)
    ang = jnp.concatenate([row[:, None] * inv, col[:, None] * inv], axis=-1)
    ang = jnp.concatenate([ang, ang], axis=-1).T
    sign_a = jnp.where(jnp.arange(HEAD_DIM) < half, -1.0, 1.0).astype(F32)[:, None]
    tt = jnp.arange(seq, dtype=F32)
    invb = ROPE_THETA ** (-jnp.arange(0, PARTIAL_ROT, 2, dtype=F32) / PARTIAL_ROT)
    angb = tt[:, None] * invb
    angb = jnp.concatenate([angb, angb], axis=-1).T
    sign_b = jnp.where(jnp.arange(PARTIAL_ROT) < PARTIAL_ROT // 2, -1.0, 1.0).astype(F32)[:, None]
    return jnp.cos(ang), jnp.sin(ang) * sign_a, jnp.cos(angb), jnp.sin(angb) * sign_b


def _trunk(x, p):
    bsz, s, d = x.shape
    t = bsz * s
    qa, qb, va, vb, ka, kb = _in_projection(x, p["g_attn"], p["w_in_t"], _rope_tables(s), p["qg"], p["kg"])
    oa = _attention_a(qa, ka, va)
    ob = _attention_b(p["lam"], qb, kb, vb, p["subln"])
    x2, xn, te, tg = _out_projection(oa, ob, x, p["wa"], p["wb"], p["g_ffn"], p["rw_t"], p["rb"])
    top_e = jnp.swapaxes(te, 1, 2).reshape(t, TOP_K)
    gates = jnp.swapaxes(tg, 1, 2).reshape(t, TOP_K)
    block_e, n_used, src_tok, dst_row, n_out_rows = _route(top_e)
    y = _moe_experts(block_e, n_used, src_tok, dst_row, n_out_rows, xn,
                     p["wg"], p["bg"], p["wu"], p["bu"], p["wd"], p["bd"])
    out = _combine(y, x2.reshape(t, d), gates, p["g_final"])
    return out.reshape(bsz, s, d)


def kernel(x_prompt, x_sample, attn_norm, w_in, a_q_norm, a_k_norm, b_lambda_q1, b_lambda_k1, b_lambda_q2,
           b_lambda_k2, b_subln, w_out, ffn_norm, router_w, router_b, w_gate, b_gate, w_up, b_up, w_down,
           b_down, final_norm):
    tm = TOKEN_TILE
    lam = (jnp.exp(jnp.sum(b_lambda_q1[0].astype(F32) * b_lambda_k1[0].astype(F32)))
           - jnp.exp(jnp.sum(b_lambda_q2[0].astype(F32) * b_lambda_k2[0].astype(F32)))
           + LAMBDA_INIT)
    p = {
        "g_attn": attn_norm[0].reshape(1, D_MODEL),
        "w_in_t": w_in[0].T.astype(BF16),
        "qg": jnp.broadcast_to(a_q_norm[0][:, None], (HEAD_DIM, tm)),
        "kg": jnp.broadcast_to(a_k_norm[0][:, None], (HEAD_DIM, tm)),
        "lam": lam.reshape(1).astype(F32),
        "subln": jnp.broadcast_to(b_subln[0][:, None], (B_V_DIM, B_Q_BLOCK)),
        "wa": w_out[0, :A_Q_W].astype(BF16),
        "wb": w_out[0, A_Q_W:].astype(BF16),
        "g_ffn": ffn_norm[0].reshape(1, D_MODEL),
        "rw_t": router_w[0].T,
        "rb": jnp.broadcast_to(router_b[0][:, None], (N_EXPERTS, tm)),
        "wg": w_gate[0].astype(BF16),
        "bg": b_gate[0].reshape(N_EXPERTS, 1, -1),
        "wu": w_up[0].astype(BF16),
        "bu": b_up[0].reshape(N_EXPERTS, 1, -1),
        "wd": w_down[0].astype(BF16),
        "bd": b_down[0].reshape(N_EXPERTS, 1, -1),
        "g_final": final_norm.reshape(1, D_MODEL),
    }
    return _trunk(x_prompt, p), _trunk(x_sample, p)
```

```python
import functools
import math

import jax
import jax.numpy as jnp
from jax import lax
from jax.experimental import pallas as pl
from jax.experimental.pallas import tpu as pltpu

F32 = jnp.float32
BF16 = jnp.bfloat16

D_MODEL = 1024
HEAD_DIM = 64
A_HEADS = 8
A_KV_HEADS = 2
A_GROUP = A_HEADS // A_KV_HEADS
B_HEADS = 4
B_V_DIM = 2 * HEAD_DIM
A_Q_W = A_HEADS * HEAD_DIM
A_KV_W = A_KV_HEADS * HEAD_DIM
B_QK_W = B_HEADS * 2 * HEAD_DIM
B_V_W = B_HEADS * B_V_DIM
IN_W = A_Q_W + 2 * A_KV_W + 2 * B_QK_W + B_V_W
GRID_W = 64
AXIAL_THETA = 10000.0
ROPE_THETA = 500000.0
PARTIAL_ROT = HEAD_DIM // 4
N_EXPERTS = 32
TOP_K = 4
SWIGLU_LIMIT = 7.0
SWIGLU_ALPHA = 1.702
EPS = 1e-6
LAMBDA_INIT = 0.8 - 0.6 * math.exp(-0.3 * 0)
Q_PRESCALE = HEAD_DIM ** -0.5 * math.log2(math.e)

O_AQ = 0
O_AK = O_AQ + A_Q_W
O_AV = O_AK + A_KV_W
O_BQ = O_AV + A_KV_W
O_BK = O_BQ + B_QK_W
O_BV = O_BK + B_QK_W

LANES = 128
TILE_SUBLANES = 8
BF16_SUBLANES = 16
ONES_ROWS = BF16_SUBLANES
VA_ROWS = HEAD_DIM + ONES_ROWS
VB_ROWS = B_V_DIM + ONES_ROWS
TOKEN_TILE = 512
A_Q_BLOCK = 256
B_Q_BLOCK = 1024
KV_CHUNK = 256
CHUNKS_PER_ITER = 8
COL_GROUP = 256
MOE_ROWS = 256
K_SPLIT = 4
N_SPLIT = 4
FIRST_GROUP_DIV = 16
COMBINE_TOKENS = 256
VMEM_LIMIT = 48 * 1024 * 1024


def _cparams(sem):
    return pltpu.CompilerParams(dimension_semantics=sem, vmem_limit_bytes=VMEM_LIMIT)


def _inproj_kernel(x_ref, g_ref, w_ref, cosa_ref, sina_ref, cosb_ref, sinb_ref, qg_ref, kg_ref,
                   qa_ref, qb_ref, va_ref, vb_ref, ka_ref, kb_ref):
    x = x_ref[0]
    tm = x.shape[0]
    ms = jnp.mean(x * x, axis=-1, keepdims=True)
    n = (x * lax.rsqrt(ms + EPS) * g_ref[...]).astype(BF16)
    pt = lax.dot_general(w_ref[...], n, (((1,), (1,)), ((), ())), preferred_element_type=F32)

    cosa = cosa_ref[...]
    sina = sina_ref[...]
    cosb = cosb_ref[...]
    sinb = sinb_ref[...]
    half = HEAD_DIM // 2
    hrot = PARTIAL_ROT // 2

    def norm_rope_a(t, gain):
        y = t * lax.rsqrt(jnp.mean(t * t, axis=0, keepdims=True) + EPS) * gain
        rot = jnp.concatenate([y[half:], y[:half]], axis=0)
        return y * cosa + rot * sina

    def rope_b(t):
        head = t[:PARTIAL_ROT]
        rot = jnp.concatenate([head[hrot:], head[:hrot]], axis=0)
        return jnp.concatenate([head * cosb + rot * sinb, t[PARTIAL_ROT:]], axis=0)

    qg = qg_ref[...]
    kg = kg_ref[...]
    ones = jnp.ones((ONES_ROWS, tm), BF16)
    for h in range(A_HEADS):
        t = pt[O_AQ + h * HEAD_DIM:O_AQ + (h + 1) * HEAD_DIM]
        qa_ref[0, h * HEAD_DIM:(h + 1) * HEAD_DIM, :] = (norm_rope_a(t, qg) * Q_PRESCALE).astype(BF16)
    for h in range(A_KV_HEADS):
        va_ref[0, h * VA_ROWS:h * VA_ROWS + HEAD_DIM, :] = (
            pt[O_AV + h * HEAD_DIM:O_AV + (h + 1) * HEAD_DIM].astype(BF16))
        va_ref[0, h * VA_ROWS + HEAD_DIM:(h + 1) * VA_ROWS, :] = ones
    for j in range(B_QK_W // HEAD_DIM):
        t = pt[O_BQ + j * HEAD_DIM:O_BQ + (j + 1) * HEAD_DIM]
        qb_ref[0, j * HEAD_DIM:(j + 1) * HEAD_DIM, :] = (rope_b(t) * Q_PRESCALE).astype(BF16)
    for h in range(B_HEADS):
        vb_ref[0, h * VB_ROWS:h * VB_ROWS + B_V_DIM, :] = (
            pt[O_BV + h * B_V_DIM:O_BV + (h + 1) * B_V_DIM].astype(BF16))
        vb_ref[0, h * VB_ROWS + B_V_DIM:(h + 1) * VB_ROWS, :] = ones

    ka = jnp.concatenate([norm_rope_a(pt[O_AK + h * HEAD_DIM:O_AK + (h + 1) * HEAD_DIM], kg)
                          for h in range(A_KV_HEADS)], axis=0)
    ka_ref[0] = ka.T.astype(BF16)
    for h in range(B_HEADS):
        kb = jnp.concatenate([rope_b(pt[O_BK + (2 * h + c) * HEAD_DIM:O_BK + (2 * h + c + 1) * HEAD_DIM])
                              for c in range(2)], axis=0)
        kb_ref[0, h] = kb.T.astype(BF16)


def _in_projection(x, g_attn, w_in_t, tabs, qg, kg):
    bsz, s, d = x.shape
    tm = TOKEN_TILE
    cosa, sina, cosb, sinb = tabs
    const = lambda shape: pl.BlockSpec(shape, lambda b, i: (0,) * len(shape))
    rows = lambda r: pl.BlockSpec((1, r, tm), lambda b, i: (b, 0, i))
    return pl.pallas_call(
        _inproj_kernel,
        grid=(bsz, s // tm),
        in_specs=[
            pl.BlockSpec((1, tm, d), lambda b, i: (b, i, 0)),
            const((1, d)),
            const((IN_W, d)),
            pl.BlockSpec((HEAD_DIM, tm), lambda b, i: (0, i)),
            pl.BlockSpec((HEAD_DIM, tm), lambda b, i: (0, i)),
            pl.BlockSpec((PARTIAL_ROT, tm), lambda b, i: (0, i)),
            pl.BlockSpec((PARTIAL_ROT, tm), lambda b, i: (0, i)),
            const((HEAD_DIM, tm)),
            const((HEAD_DIM, tm)),
        ],
        out_specs=[
            rows(A_Q_W), rows(B_QK_W), rows(A_KV_HEADS * VA_ROWS), rows(B_HEADS * VB_ROWS),
            pl.BlockSpec((1, tm, LANES), lambda b, i: (b, i, 0)),
            pl.BlockSpec((1, B_HEADS, tm, LANES), lambda b, i: (b, 0, i, 0)),
        ],
        out_shape=[
            jax.ShapeDtypeStruct((bsz, A_Q_W, s), BF16),
            jax.ShapeDtypeStruct((bsz, B_QK_W, s), BF16),
            jax.ShapeDtypeStruct((bsz, A_KV_HEADS * VA_ROWS, s), BF16),
            jax.ShapeDtypeStruct((bsz, B_HEADS * VB_ROWS, s), BF16),
            jax.ShapeDtypeStruct((bsz, s, LANES), BF16),
            jax.ShapeDtypeStruct((bsz, B_HEADS, s, LANES), BF16),
        ],
        compiler_params=_cparams(("parallel", "parallel")),
        name="in_projection",
    )(x, g_attn, w_in_t, cosa, sina, cosb, sinb, qg, kg)


def _column_softmax_attention(k_ref, rhs_ref, v_ref, v_rows, acc_ref, m_ref, s_refs, seq):
    n = rhs_ref.shape[1]
    bk = KV_CHUNK
    gw = COL_GROUP
    n_groups = n // gw
    n_iters = seq // (CHUNKS_PER_ITER * bk)

    def scores(c, s_ref, g):
        off = pl.multiple_of(c * bk, bk)
        s_ref[:, g * gw:(g + 1) * gw] = jnp.dot(k_ref[pl.ds(off, bk), :], rhs_ref[:, g * gw:(g + 1) * gw],
                                                preferred_element_type=F32)

    def softmax_pv(c, s_ref, g):
        off = pl.multiple_of(c * bk, bk)
        cols = slice(g * gw, (g + 1) * gw)
        m_old = m_ref[:, cols]
        m_parts, p_parts = [], []
        for j in range(gw // LANES):
            lo = g * gw + j * LANES
            sj = s_ref[:, lo:lo + LANES]
            mj = jnp.maximum(m_old[:, j * LANES:(j + 1) * LANES], jnp.max(sj, axis=0, keepdims=True))
            p_parts.append(jnp.exp2((sj - mj).astype(BF16)))
            m_parts.append(mj)
        m_new = jnp.concatenate(m_parts, axis=1)
        m_ref[:, cols] = m_new
        alpha = jnp.exp2(m_old - m_new)
        rl, rh = v_rows(g)
        pv = jnp.dot(v_ref[rl:rh, pl.ds(off, bk)], jnp.concatenate(p_parts, axis=1),
                     preferred_element_type=F32)
        acc_ref[:, cols] = alpha * acc_ref[:, cols] + pv

    def half(c, cur, nxt, issue_next):
        for g in range(n_groups):
            if issue_next:
                scores(c + 1, nxt, g)
            softmax_pv(c, cur, g)

    def chunks(j, last):
        for u in range(CHUNKS_PER_ITER):
            half(j * CHUNKS_PER_ITER + u, s_refs[u % 2], s_refs[(u + 1) % 2],
                 not (last and u == CHUNKS_PER_ITER - 1))

    acc_ref[...] = jnp.zeros_like(acc_ref)
    m_ref[...] = jnp.full(m_ref.shape, -jnp.inf, F32)
    for g in range(n_groups):
        scores(0, s_refs[0], g)

    def body(j, carry):
        chunks(j, False)
        return carry

    lax.fori_loop(0, n_iters - 1, body, 0)
    chunks(n_iters - 1, True)


def _attn_a_kernel(q_ref, k_ref, v_ref, o_ref, rhs_ref, acc_ref, m_ref, s0, s1, *, seq):
    bq = A_Q_BLOCK
    half_n = A_GROUP * bq
    rhs_ref[...] = jnp.zeros_like(rhs_ref)
    for h in range(A_HEADS):
        kv = h // A_GROUP
        rhs_ref[kv * HEAD_DIM:(kv + 1) * HEAD_DIM, h * bq:(h + 1) * bq] = q_ref[0, h * HEAD_DIM:(h + 1) * HEAD_DIM, :]
    v_rows = lambda g: ((g * COL_GROUP // half_n) * VA_ROWS, (g * COL_GROUP // half_n + 1) * VA_ROWS)
    _column_softmax_attention(k_ref.at[0], rhs_ref, v_ref.at[0], v_rows, acc_ref, m_ref, (s0, s1), seq)
    acc = acc_ref[...]
    o = acc[:HEAD_DIM] * (1.0 / acc[HEAD_DIM:HEAD_DIM + 1])
    for h in range(A_HEADS):
        o_ref[0, h * HEAD_DIM:(h + 1) * HEAD_DIM, :] = o[:, h * bq:(h + 1) * bq].astype(BF16)


def _attention_scratch(acc_rows, n):
    return [pltpu.VMEM((LANES, n), BF16), pltpu.VMEM((acc_rows, n), F32), pltpu.VMEM((1, n), F32),
            pltpu.VMEM((KV_CHUNK, n), F32), pltpu.VMEM((KV_CHUNK, n), F32)]


def _attention_a(qa, ka, va):
    bsz, _, s = qa.shape
    bq = A_Q_BLOCK
    assert s % (CHUNKS_PER_ITER * KV_CHUNK) == 0 and s % bq == 0
    n = A_HEADS * bq
    return pl.pallas_call(
        functools.partial(_attn_a_kernel, seq=s),
        grid=(bsz, s // bq),
        in_specs=[
            pl.BlockSpec((1, A_Q_W, bq), lambda b, i: (b, 0, i)),
            pl.BlockSpec((1, s, LANES), lambda b, i: (b, 0, 0)),
            pl.BlockSpec((1, A_KV_HEADS * VA_ROWS, s), lambda b, i: (b, 0, 0)),
        ],
        out_specs=pl.BlockSpec((1, A_Q_W, bq), lambda b, i: (b, 0, i)),
        out_shape=jax.ShapeDtypeStruct((bsz, A_Q_W, s), BF16),
        scratch_shapes=_attention_scratch(VA_ROWS, n),
        compiler_params=_cparams(("parallel", "parallel")),
        name="attention_a",
    )(qa, ka, va)


def _attn_b_kernel(lam_ref, q_ref, k_ref, v_ref, sg_ref, o_ref, rhs_ref, acc, m_ref, s0, s1, *, seq):
    bq = B_Q_BLOCK
    rhs_ref[...] = jnp.zeros_like(rhs_ref)
    rhs_ref[:HEAD_DIM, :bq] = q_ref[0, :HEAD_DIM, :]
    rhs_ref[HEAD_DIM:, bq:] = q_ref[0, HEAD_DIM:, :]
    _column_softmax_attention(k_ref.at[0, 0], rhs_ref, v_ref.at[0], lambda g: (0, VB_ROWS), acc, m_ref,
                              (s0, s1), seq)
    a = acc[...]
    o = a[:B_V_DIM] * (1.0 / a[B_V_DIM:B_V_DIM + 1])
    o = o[:, :bq] - lam_ref[0] * o[:, bq:]
    o = o * lax.rsqrt(jnp.mean(o * o, axis=0, keepdims=True) + EPS) * sg_ref[...]
    o_ref[0] = (o * (1.0 - LAMBDA_INIT)).astype(BF16)


def _attention_b(lam, qb, kb, vb, subln):
    bsz, _, s = qb.shape
    bq = B_Q_BLOCK
    assert s % (CHUNKS_PER_ITER * KV_CHUNK) == 0 and s % bq == 0
    n = 2 * bq
    return pl.pallas_call(
        functools.partial(_attn_b_kernel, seq=s),
        grid=(bsz, B_HEADS, s // bq),
        in_specs=[
            pl.BlockSpec(memory_space=pltpu.SMEM),
            pl.BlockSpec((1, B_V_DIM, bq), lambda b, h, i: (b, h, i)),
            pl.BlockSpec((1, 1, s, LANES), lambda b, h, i: (b, h, 0, 0)),
            pl.BlockSpec((1, VB_ROWS, s), lambda b, h, i: (b, h, 0)),
            pl.BlockSpec((B_V_DIM, bq), lambda b, h, i: (0, 0)),
        ],
        out_specs=pl.BlockSpec((1, B_V_DIM, bq), lambda b, h, i: (b, h, i)),
        out_shape=jax.ShapeDtypeStruct((bsz, B_V_W, s), BF16),
        scratch_shapes=_attention_scratch(VB_ROWS, n),
        compiler_params=_cparams(("parallel", "parallel", "parallel")),
        name="attention_b",
    )(lam, qb, kb, vb, subln)


def _store_token_tiles(ref, val):
    n = val.shape[0]
    for j in range(TILE_SUBLANES):
        ref[pl.ds(j, n, stride=TILE_SUBLANES), :] = val[:, j * LANES:(j + 1) * LANES]


def _load_token_tiles(ref, n):
    return jnp.concatenate([ref[pl.ds(j, n, stride=TILE_SUBLANES), :] for j in range(TILE_SUBLANES)], axis=1)


def _outproj_kernel(a_ref, b_ref, x_ref, wa_ref, wb_ref, g_ref, rw_ref, rb_ref,
                    x2_ref, xn_ref, te_ref, tg_ref):
    tn = (((0,), (0,)), ((), ()))
    y = (x_ref[0]
         + lax.dot_general(a_ref[0], wa_ref[...], tn, preferred_element_type=F32)
         + lax.dot_general(b_ref[0], wb_ref[...], tn, preferred_element_type=F32))
    x2_ref[0] = y
    xn = y * lax.rsqrt(jnp.mean(y * y, axis=-1, keepdims=True) + EPS) * g_ref[...]
    _store_token_tiles(xn_ref, xn)
    logits = lax.dot_general(rw_ref[...], xn, (((1,), (1,)), ((), ())),
                             precision=lax.Precision.HIGHEST, preferred_element_type=F32) + rb_ref[...]
    iota = lax.broadcasted_iota(jnp.int32, logits.shape, 0)
    work = logits
    vals, idxs = [], []
    for _ in range(TOP_K):
        mx = jnp.max(work, axis=0, keepdims=True)
        idx = jnp.min(jnp.where(work == mx, iota, N_EXPERTS), axis=0, keepdims=True)
        vals.append(mx)
        idxs.append(idx)
        work = jnp.where(iota == idx, -jnp.inf, work)
    ex = [jnp.exp(v - vals[0]) for v in vals]
    inv = 1.0 / (ex[0] + ex[1] + ex[2] + ex[3])
    te_ref[0] = jnp.concatenate(idxs, axis=0)
    tg_ref[0] = jnp.concatenate([e * inv for e in ex], axis=0)


def _out_projection(oa, ob, x, wa, wb, g_ffn, rw_t, rb):
    bsz, s, d = x.shape
    tm = TOKEN_TILE
    const = lambda shape: pl.BlockSpec(shape, lambda b, i: (0,) * len(shape))
    tok = pl.BlockSpec((1, tm, d), lambda b, i: (b, i, 0))
    sel = pl.BlockSpec((1, TOP_K, tm), lambda b, i: (b, 0, i))
    return pl.pallas_call(
        _outproj_kernel,
        grid=(bsz, s // tm),
        in_specs=[
            pl.BlockSpec((1, A_Q_W, tm), lambda b, i: (b, 0, i)),
            pl.BlockSpec((1, B_V_W, tm), lambda b, i: (b, 0, i)),
            tok,
            const((A_Q_W, d)),
            const((B_V_W, d)),
            const((1, d)),
            const((N_EXPERTS, d)),
            const((N_EXPERTS, tm)),
        ],
        out_specs=[tok,
                   pl.BlockSpec((tm * TILE_SUBLANES, LANES), lambda b, i: (b * (s // tm) + i, 0)),
                   sel, sel],
        out_shape=[
            jax.ShapeDtypeStruct((bsz, s, d), F32),
            jax.ShapeDtypeStruct((bsz * s * TILE_SUBLANES, LANES), F32),
            jax.ShapeDtypeStruct((bsz, TOP_K, s), jnp.int32),
            jax.ShapeDtypeStruct((bsz, TOP_K, s), F32),
        ],
        compiler_params=_cparams(("parallel", "parallel")),
        name="out_projection_router",
    )(oa, ob, x, wa, wb, g_ffn, rw_t, rb)


def _tile_rows(r):
    return pl.ds(r * TILE_SUBLANES, TILE_SUBLANES)


def _gather_row(idx_ref, r, src_hbm, buf, sem):
    src = pl.ds(pl.multiple_of(idx_ref[0, 0, r], TILE_SUBLANES), TILE_SUBLANES)
    return pltpu.make_async_copy(src_hbm.at[src, :], buf.at[_tile_rows(r), :], sem)


def _scatter_row(idx_ref, r, buf, dst_hbm, sem):
    dst = pl.ds(pl.multiple_of(idx_ref[0, 0, r], TILE_SUBLANES), TILE_SUBLANES)
    return pltpu.make_async_copy(buf.at[_tile_rows(r), :], dst_hbm.at[dst, :], sem)


def _copy_groups(rows, parts):
    first = rows // FIRST_GROUP_DIV
    rest = (rows - first) // (parts - 1)
    return [0] + [first + k * rest for k in range(parts - 1)] + [rows]


def _moe_kernel(be_ref, nb_ref, tok_first, tok_nxt, dst_prev, xn_hbm, wg_ref, bg_ref, wu_ref, bu_ref,
                wd_ref, bd_ref, y_hbm, xbuf_all, ybuf_all, gsem, ssem, *, pad_base):
    i = pl.program_id(0)
    n_used = nb_ref[0]
    rows = MOE_ROWS
    tile_rows = rows * TILE_SUBLANES
    xbufs = tuple(xbuf_all.at[pl.ds(s * tile_rows, tile_rows), :] for s in range(2))
    ybufs = tuple(ybuf_all.at[pl.ds(s * tile_rows, tile_rows), :] for s in range(2))

    @pl.when(i == 0)
    def _():
        ybuf_all[...] = jnp.zeros_like(ybuf_all)
        fills = [pltpu.make_async_copy(ybufs[0], y_hbm.at[pl.ds(pad_base + j * tile_rows, tile_rows), :],
                                       ssem.at[0]) for j in range(N_EXPERTS)]
        for f in fills:
            f.start()
        for f in fills:
            f.wait()
        fills[1].start()
        for r in range(rows):
            _gather_row(tok_first, r, xn_hbm, xbufs[0], gsem.at[0]).start(priority=r % 2)

    def step(slot):
        xbuf, ybuf = xbufs[slot], ybufs[slot]
        y_prev, sem_prev = ybufs[1 - slot], ssem.at[1 - slot]

        @pl.when(i <= n_used)
        def _():
            pltpu.make_async_copy(xn_hbm.at[pl.ds(0, tile_rows), :], xbuf, gsem.at[slot]).wait()

        def wait_scatter():
            pltpu.make_async_copy(ybuf, y_hbm.at[pl.ds(0, tile_rows), :], ssem.at[slot]).wait()

        pl.when(jnp.logical_and(i >= n_used, i - 2 < n_used))(wait_scatter)

        @pl.when(i == n_used)
        def _():
            for r in range(rows):
                _scatter_row(dst_prev, r, y_prev, y_hbm, sem_prev).start(priority=r % 2)

        @pl.when(i < n_used)
        def _():
            kb = _copy_groups(rows, K_SPLIT)
            kw = D_MODEL // K_SPLIT
            g = bg_ref[0]
            u = bu_ref[0]
            for q in range(K_SPLIT):
                for r in range(kb[q], kb[q + 1]):
                    _gather_row(tok_nxt, r, xn_hbm, xbufs[1 - slot], gsem.at[1 - slot]).start(priority=r % 2)
                xq = jnp.concatenate([xbuf[pl.ds(j, rows, stride=TILE_SUBLANES), :]
                                      for j in range(q * kw // LANES, (q + 1) * kw // LANES)], axis=1).astype(BF16)
                g = g + jnp.dot(xq, wg_ref[0, q * kw:(q + 1) * kw, :], preferred_element_type=F32)
                u = u + jnp.dot(xq, wu_ref[0, q * kw:(q + 1) * kw, :], preferred_element_type=F32)
            g = jnp.minimum(g, SWIGLU_LIMIT)
            u = jnp.clip(u, -SWIGLU_LIMIT, SWIGLU_LIMIT)
            hb = (g * jax.nn.sigmoid(g * SWIGLU_ALPHA) * (u + 1.0)).astype(BF16)
            wait_scatter()
            nb = _copy_groups(rows, N_SPLIT)
            nw = D_MODEL // N_SPLIT
            for q in range(N_SPLIT):
                for r in range(nb[q], nb[q + 1]):
                    _scatter_row(dst_prev, r, y_prev, y_hbm, sem_prev).start(priority=r % 2)
                cols = slice(q * nw, (q + 1) * nw)
                yq = jnp.dot(hb, wd_ref[0, :, cols], preferred_element_type=F32) + bd_ref[0, :, cols]
                for j in range(nw // LANES):
                    ybuf[pl.ds(q * nw // LANES + j, rows, stride=TILE_SUBLANES), :] = yq[:, j * LANES:(j + 1) * LANES]

    for slot in range(2):
        pl.when(lax.rem(i, 2) == slot)(functools.partial(step, slot))


def _moe_experts(block_e, n_used, src_tok, dst_row, n_out_rows, xn, wg, bg, wu, bu, wd, bd):
    d = D_MODEL
    t = xn.shape[0] // TILE_SUBLANES
    bm = MOE_ROWS
    n_blocks = src_tok.shape[0]
    dff = wg.shape[2]
    last = n_blocks - 1
    n_steps = n_blocks + 2
    blk = lambda i: jnp.minimum(i, last)
    wspec = lambda k, n: pl.BlockSpec((1, k, n), lambda i, be, nb: (be[blk(i)], 0, 0))
    idx = lambda fn: pl.BlockSpec((1, 1, bm), lambda i, be, nb: (fn(i), 0, 0), memory_space=pltpu.SMEM)
    grid_spec = pltpu.PrefetchScalarGridSpec(
        num_scalar_prefetch=2,
        grid=(n_steps,),
        in_specs=[
            idx(lambda i: 0), idx(lambda i: blk(i + 1)), idx(lambda i: jnp.minimum(i, n_blocks)),
            pl.BlockSpec(memory_space=pl.ANY),
            wspec(d, dff), wspec(1, dff), wspec(d, dff), wspec(1, dff), wspec(dff, d), wspec(1, d),
        ],
        out_specs=pl.BlockSpec(memory_space=pl.ANY),
        scratch_shapes=[pltpu.VMEM((2 * bm * TILE_SUBLANES, LANES), F32)] * 2
                       + [pltpu.SemaphoreType.DMA((2,)), pltpu.SemaphoreType.DMA((2,))],
    )
    return pl.pallas_call(
        functools.partial(_moe_kernel, pad_base=t * TOP_K * TILE_SUBLANES),
        grid_spec=grid_spec,
        out_shape=jax.ShapeDtypeStruct((n_out_rows * TILE_SUBLANES, LANES), F32),
        compiler_params=_cparams(("arbitrary",)),
        name="moe_experts",
    )(block_e, n_used, src_tok, src_tok, dst_row, xn, wg, bg, wu, bu, wd, bd)


def _combine_kernel(y0_ref, y1_ref, y2_ref, y3_ref, x2_ref, gate_ref, g_ref, o_ref):
    tc = x2_ref.shape[0]
    gates = gate_ref[...]
    y = x2_ref[...]
    for k, y_ref in enumerate((y0_ref, y1_ref, y2_ref, y3_ref)):
        y = y + gates[:, k:k + 1] * _load_token_tiles(y_ref, tc)
    o_ref[...] = y * lax.rsqrt(jnp.mean(y * y, axis=-1, keepdims=True) + EPS) * g_ref[...]


def _combine(y, x2, gates, g_final):
    t, d = x2.shape
    tc = COMBINE_TOKENS
    n = t // tc
    yspec = lambda k: pl.BlockSpec((tc * TILE_SUBLANES, LANES), lambda i: (k * n + i, 0))
    return pl.pallas_call(
        _combine_kernel,
        grid=(n,),
        in_specs=[
            yspec(0), yspec(1), yspec(2), yspec(3),
            pl.BlockSpec((tc, d), lambda i: (i, 0)),
            pl.BlockSpec((tc, TOP_K), lambda i: (i, 0)),
            pl.BlockSpec((1, d), lambda i: (0, 0)),
        ],
        out_specs=pl.BlockSpec((tc, d), lambda i: (i, 0)),
        out_shape=jax.ShapeDtypeStruct((t, d), F32),
        compiler_params=_cparams(("parallel",)),
        name="combine_final_norm",
    )(y, y, y, y, x2, gates, g_final)


def _route(top_e):
    t = top_e.shape[0]
    a = t * TOP_K
    bm = MOE_ROWS
    n_blocks = a // bm + N_EXPERTS
    flat_e = top_e.reshape(-1)
    order = jnp.argsort(flat_e, stable=True).astype(jnp.int32)
    counts = jnp.bincount(flat_e, length=N_EXPERTS).astype(jnp.int32)
    padded = ((counts + bm - 1) // bm) * bm
    start = jnp.cumsum(counts) - counts
    pend = jnp.cumsum(padded)
    pstart = pend - padded
    block_start = jnp.arange(n_blocks, dtype=jnp.int32) * bm
    block_e = jnp.minimum(jnp.sum((pend[None, :] <= block_start[:, None]).astype(jnp.int32), axis=1),
                          N_EXPERTS - 1)
    rows = jnp.arange(n_blocks * bm, dtype=jnp.int32)
    row_e = jnp.repeat(block_e, bm)
    within = rows - pstart[row_e]
    valid = within < counts[row_e]
    assign = order[jnp.clip(start[row_e] + within, 0, a - 1)]
    pad_before = jnp.cumsum(padded - counts) - (padded - counts)
    pad_rank = jnp.minimum(pad_before[row_e] + within - counts[row_e], N_EXPERTS * bm - 1)
    tok = assign // TOP_K
    src = (jnp.where(valid, tok, 0) * TILE_SUBLANES).astype(jnp.int32)
    dst = (jnp.where(valid, (assign % TOP_K) * t + tok, a + pad_rank) * TILE_SUBLANES).astype(jnp.int32)
    dummy = (a + jnp.arange(bm, dtype=jnp.int32)) * TILE_SUBLANES
    dst = jnp.concatenate([dummy, dst]).reshape(n_blocks + 1, 1, bm)
    n_used = (pend[-1] // bm).astype(jnp.int32).reshape(1)
    return block_e, n_used, src.reshape(n_blocks, 1, bm), dst, a + N_EXPERTS * bm


def _rope_tables(seq):
    rows = seq // GRID_W
    row = jnp.repeat(jnp.arange(rows, dtype=F32), GRID_W)
    col = jnp.tile(jnp.arange(GRID_W, dtype=F32), rows)
    half = HEAD_DIM // 2
    inv = AXIAL_THETA ** (-jnp.arange(0, half, 2, dtype=F32) / half)
    ang = jnp.concatenate([row[:, None] * inv, col[:, None] * inv], axis=-1)
    ang = jnp.concatenate([ang, ang], axis=-1).T
    sign_a = jnp.where(jnp.arange(HEAD_DIM) < half, -1.0, 1.0).astype(F32)[:, None]
    tt = jnp.arange(seq, dtype=F32)
    invb = ROPE_THETA ** (-jnp.arange(0, PARTIAL_ROT, 2, dtype=F32) / PARTIAL_ROT)
    angb = tt[:, None] * invb
    angb = jnp.concatenate([angb, angb], axis=-1).T
    sign_b = jnp.where(jnp.arange(PARTIAL_ROT) < PARTIAL_ROT // 2, -1.0, 1.0).astype(F32)[:, None]
    return jnp.cos(ang), jnp.sin(ang) * sign_a, jnp.cos(angb), jnp.sin(angb) * sign_b


def _trunk(x, p):
    bsz, s, d = x.shape
    t = bsz * s
    qa, qb, va, vb, ka, kb = _in_projection(x, p["g_attn"], p["w_in_t"], _rope_tables(s), p["qg"], p["kg"])
    oa = _attention_a(qa, ka, va)
    ob = _attention_b(p["lam"], qb, kb, vb, p["subln"])
    x2, xn, te, tg = _out_projection(oa, ob, x, p["wa"], p["wb"], p["g_ffn"], p["rw_t"], p["rb"])
    top_e = jnp.swapaxes(te, 1, 2).reshape(t, TOP_K)
    gates = jnp.swapaxes(tg, 1, 2).reshape(t, TOP_K)
    block_e, n_used, src_tok, dst_row, n_out_rows = _route(top_e)
    y = _moe_experts(block_e, n_used, src_tok, dst_row, n_out_rows, xn,
                     p["wg"], p["bg"], p["wu"], p["bu"], p["wd"], p["bd"])
    out = _combine(y, x2.reshape(t, d), gates, p["g_final"])
    return out.reshape(bsz, s, d)


def kernel(x_prompt, x_sample, attn_norm, w_in, a_q_norm, a_k_norm, b_lambda_q1, b_lambda_k1, b_lambda_q2,
           b_lambda_k2, b_subln, w_out, ffn_norm, router_w, router_b, w_gate, b_gate, w_up, b_up, w_down,
           b_down, final_norm):
    tm = TOKEN_TILE
    lam = (jnp.exp(jnp.sum(b_lambda_q1[0].astype(F32) * b_lambda_k1[0].astype(F32)))
           - jnp.exp(jnp.sum(b_lambda_q2[0].astype(F32) * b_lambda_k2[0].astype(F32)))
           + LAMBDA_INIT)
    p = {
        "g_attn": attn_norm[0].reshape(1, D_MODEL),
        "w_in_t": w_in[0].T.astype(BF16),
        "qg": jnp.broadcast_to(a_q_norm[0][:, None], (HEAD_DIM, tm)),
        "kg": jnp.broadcast_to(a_k_norm[0][:, None], (HEAD_DIM, tm)),
        "lam": lam.reshape(1).astype(F32),
        "subln": jnp.broadcast_to(b_subln[0][:, None], (B_V_DIM, B_Q_BLOCK)),
        "wa": w_out[0, :A_Q_W].astype(BF16),
        "wb": w_out[0, A_Q_W:].astype(BF16),
        "g_ffn": ffn_norm[0].reshape(1, D_MODEL),
        "rw_t": router_w[0].T,
        "rb": jnp.broadcast_to(router_b[0][:, None], (N_EXPERTS, tm)),
        "wg": w_gate[0].astype(BF16),
        "bg": b_gate[0].reshape(N_EXPERTS, 1, -1),
        "wu": w_up[0].astype(BF16),
        "bu": b_up[0].reshape(N_EXPERTS, 1, -1),
        "wd": w_down[0].astype(BF16),
        "bd": b_down[0].reshape(N_EXPERTS, 1, -1),
        "g_final": final_norm.reshape(1, D_MODEL),
    }
    return _trunk(x_prompt, p), _trunk(x_sample, p)
```

```python
import functools
import math

import jax
import jax.numpy as jnp
from jax import lax
from jax.experimental import pallas as pl
from jax.experimental.pallas import tpu as pltpu

F32 = jnp.float32
BF16 = jnp.bfloat16

D_MODEL = 1024
HEAD_DIM = 64
A_HEADS = 8
A_KV_HEADS = 2
A_GROUP = A_HEADS // A_KV_HEADS
B_HEADS = 4
B_V_DIM = 2 * HEAD_DIM
A_Q_W = A_HEADS * HEAD_DIM
A_KV_W = A_KV_HEADS * HEAD_DIM
B_QK_W = B_HEADS * 2 * HEAD_DIM
B_V_W = B_HEADS * B_V_DIM
IN_W = A_Q_W + 2 * A_KV_W + 2 * B_QK_W + B_V_W
GRID_W = 64
AXIAL_THETA = 10000.0
ROPE_THETA = 500000.0
PARTIAL_ROT = HEAD_DIM // 4
N_EXPERTS = 32
TOP_K = 4
SWIGLU_LIMIT = 7.0
SWIGLU_ALPHA = 1.702
EPS = 1e-6
LAMBDA_INIT = 0.8 - 0.6 * math.exp(-0.3 * 0)
Q_PRESCALE = HEAD_DIM ** -0.5 * math.log2(math.e)

O_AQ = 0
O_AK = O_AQ + A_Q_W
O_AV = O_AK + A_KV_W
O_BQ = O_AV + A_KV_W
O_BK = O_BQ + B_QK_W
O_BV = O_BK + B_QK_W

LANES = 128
TILE_SUBLANES = 8
BF16_SUBLANES = 16
ONES_ROWS = BF16_SUBLANES
VA_ROWS = HEAD_DIM + ONES_ROWS
VB_ROWS = B_V_DIM + ONES_ROWS
TOKEN_TILE = 512
A_Q_BLOCK = 256
B_Q_BLOCK = 1024
KV_CHUNK = 256
CHUNKS_PER_ITER = 16
COL_GROUP = 256
MOE_ROWS = 256
K_SPLIT = 4
N_SPLIT = 4
FIRST_GROUP_DIV = 16
COMBINE_TOKENS = 256
VMEM_LIMIT = 48 * 1024 * 1024


def _cparams(sem):
    return pltpu.CompilerParams(dimension_semantics=sem, vmem_limit_bytes=VMEM_LIMIT)


def _inproj_kernel(x_ref, g_ref, w_ref, cosa_ref, sina_ref, cosb_ref, sinb_ref, qg_ref, kg_ref,
                   qa_ref, qb_ref, va_ref, vb_ref, ka_ref, kb_ref):
    x = x_ref[0]
    tm = x.shape[0]
    ms = jnp.mean(x * x, axis=-1, keepdims=True)
    n = (x * lax.rsqrt(ms + EPS) * g_ref[...]).astype(BF16)
    pt = lax.dot_general(w_ref[...], n, (((1,), (1,)), ((), ())), preferred_element_type=F32)

    cosa = cosa_ref[...]
    sina = sina_ref[...]
    cosb = cosb_ref[...]
    sinb = sinb_ref[...]
    half = HEAD_DIM // 2
    hrot = PARTIAL_ROT // 2

    def norm_rope_a(t, gain):
        y = t * lax.rsqrt(jnp.mean(t * t, axis=0, keepdims=True) + EPS) * gain
        rot = jnp.concatenate([y[half:], y[:half]], axis=0)
        return y * cosa + rot * sina

    def rope_b(t):
        head = t[:PARTIAL_ROT]
        rot = jnp.concatenate([head[hrot:], head[:hrot]], axis=0)
        return jnp.concatenate([head * cosb + rot * sinb, t[PARTIAL_ROT:]], axis=0)

    qg = qg_ref[...]
    kg = kg_ref[...]
    ones = jnp.ones((ONES_ROWS, tm), BF16)
    for h in range(A_HEADS):
        t = pt[O_AQ + h * HEAD_DIM:O_AQ + (h + 1) * HEAD_DIM]
        qa_ref[0, h * HEAD_DIM:(h + 1) * HEAD_DIM, :] = (norm_rope_a(t, qg) * Q_PRESCALE).astype(BF16)
    for h in range(A_KV_HEADS):
        va_ref[0, h * VA_ROWS:h * VA_ROWS + HEAD_DIM, :] = (
            pt[O_AV + h * HEAD_DIM:O_AV + (h + 1) * HEAD_DIM].astype(BF16))
        va_ref[0, h * VA_ROWS + HEAD_DIM:(h + 1) * VA_ROWS, :] = ones
    for j in range(B_QK_W // HEAD_DIM):
        t = pt[O_BQ + j * HEAD_DIM:O_BQ + (j + 1) * HEAD_DIM]
        qb_ref[0, j * HEAD_DIM:(j + 1) * HEAD_DIM, :] = (rope_b(t) * Q_PRESCALE).astype(BF16)
    for h in range(B_HEADS):
        vb_ref[0, h * VB_ROWS:h * VB_ROWS + B_V_DIM, :] = (
            pt[O_BV + h * B_V_DIM:O_BV + (h + 1) * B_V_DIM].astype(BF16))
        vb_ref[0, h * VB_ROWS + B_V_DIM:(h + 1) * VB_ROWS, :] = ones

    ka = jnp.concatenate([norm_rope_a(pt[O_AK + h * HEAD_DIM:O_AK + (h + 1) * HEAD_DIM], kg)
                          for h in range(A_KV_HEADS)], axis=0)
    ka_ref[0] = ka.T.astype(BF16)
    for h in range(B_HEADS):
        kb = jnp.concatenate([rope_b(pt[O_BK + (2 * h + c) * HEAD_DIM:O_BK + (2 * h + c + 1) * HEAD_DIM])
                              for c in range(2)], axis=0)
        kb_ref[0, h] = kb.T.astype(BF16)


def _in_projection(x, g_attn, w_in_t, tabs, qg, kg):
    bsz, s, d = x.shape
    tm = TOKEN_TILE
    cosa, sina, cosb, sinb = tabs
    const = lambda shape: pl.BlockSpec(shape, lambda b, i: (0,) * len(shape))
    rows = lambda r: pl.BlockSpec((1, r, tm), lambda b, i: (b, 0, i))
    return pl.pallas_call(
        _inproj_kernel,
        grid=(bsz, s // tm),
        in_specs=[
            pl.BlockSpec((1, tm, d), lambda b, i: (b, i, 0)),
            const((1, d)),
            const((IN_W, d)),
            pl.BlockSpec((HEAD_DIM, tm), lambda b, i: (0, i)),
            pl.BlockSpec((HEAD_DIM, tm), lambda b, i: (0, i)),
            pl.BlockSpec((PARTIAL_ROT, tm), lambda b, i: (0, i)),
            pl.BlockSpec((PARTIAL_ROT, tm), lambda b, i: (0, i)),
            const((HEAD_DIM, tm)),
            const((HEAD_DIM, tm)),
        ],
        out_specs=[
            rows(A_Q_W), rows(B_QK_W), rows(A_KV_HEADS * VA_ROWS), rows(B_HEADS * VB_ROWS),
            pl.BlockSpec((1, tm, LANES), lambda b, i: (b, i, 0)),
            pl.BlockSpec((1, B_HEADS, tm, LANES), lambda b, i: (b, 0, i, 0)),
        ],
        out_shape=[
            jax.ShapeDtypeStruct((bsz, A_Q_W, s), BF16),
            jax.ShapeDtypeStruct((bsz, B_QK_W, s), BF16),
            jax.ShapeDtypeStruct((bsz, A_KV_HEADS * VA_ROWS, s), BF16),
            jax.ShapeDtypeStruct((bsz, B_HEADS * VB_ROWS, s), BF16),
            jax.ShapeDtypeStruct((bsz, s, LANES), BF16),
            jax.ShapeDtypeStruct((bsz, B_HEADS, s, LANES), BF16),
        ],
        compiler_params=_cparams(("parallel", "parallel")),
        name="in_projection",
    )(x, g_attn, w_in_t, cosa, sina, cosb, sinb, qg, kg)


def _column_softmax_attention(k_ref, rhs_ref, v_ref, v_rows, acc_ref, m_ref, s_refs, seq):
    n = rhs_ref.shape[1]
    bk = KV_CHUNK
    gw = COL_GROUP
    n_groups = n // gw
    n_iters = seq // (CHUNKS_PER_ITER * bk)

    def scores(c, s_ref, g):
        off = pl.multiple_of(c * bk, bk)
        s_ref[:, g * gw:(g + 1) * gw] = jnp.dot(k_ref[pl.ds(off, bk), :], rhs_ref[:, g * gw:(g + 1) * gw],
                                                preferred_element_type=F32)

    def softmax_pv(c, s_ref, g):
        off = pl.multiple_of(c * bk, bk)
        cols = slice(g * gw, (g + 1) * gw)
        m_old = m_ref[:, cols]
        m_parts, p_parts = [], []
        for j in range(gw // LANES):
            lo = g * gw + j * LANES
            sj = s_ref[:, lo:lo + LANES]
            mj = jnp.maximum(m_old[:, j * LANES:(j + 1) * LANES], jnp.max(sj, axis=0, keepdims=True))
            p_parts.append(jnp.exp2((sj - mj).astype(BF16)))
            m_parts.append(mj)
        m_new = jnp.concatenate(m_parts, axis=1)
        m_ref[:, cols] = m_new
        alpha = jnp.exp2(m_old - m_new)
        rl, rh = v_rows(g)
        pv = jnp.dot(v_ref[rl:rh, pl.ds(off, bk)], jnp.concatenate(p_parts, axis=1),
                     preferred_element_type=F32)
        acc_ref[:, cols] = alpha * acc_ref[:, cols] + pv

    def half(c, cur, nxt, issue_next):
        for g in range(n_groups):
            if issue_next:
                scores(c + 1, nxt, g)
            softmax_pv(c, cur, g)

    def chunks(j, last):
        for u in range(CHUNKS_PER_ITER):
            half(j * CHUNKS_PER_ITER + u, s_refs[u % 2], s_refs[(u + 1) % 2],
                 not (last and u == CHUNKS_PER_ITER - 1))

    acc_ref[...] = jnp.zeros_like(acc_ref)
    m_ref[...] = jnp.full(m_ref.shape, -jnp.inf, F32)
    for g in range(n_groups):
        scores(0, s_refs[0], g)

    def body(j, carry):
        chunks(j, False)
        return carry

    lax.fori_loop(0, n_iters - 1, body, 0)
    chunks(n_iters - 1, True)


def _attn_a_kernel(q_ref, k_ref, v_ref, o_ref, rhs_ref, acc_ref, m_ref, s0, s1, *, seq):
    bq = A_Q_BLOCK
    half_n = A_GROUP * bq
    rhs_ref[...] = jnp.zeros_like(rhs_ref)
    for h in range(A_HEADS):
        kv = h // A_GROUP
        rhs_ref[kv * HEAD_DIM:(kv + 1) * HEAD_DIM, h * bq:(h + 1) * bq] = q_ref[0, h * HEAD_DIM:(h + 1) * HEAD_DIM, :]
    v_rows = lambda g: ((g * COL_GROUP // half_n) * VA_ROWS, (g * COL_GROUP // half_n + 1) * VA_ROWS)
    _column_softmax_attention(k_ref.at[0], rhs_ref, v_ref.at[0], v_rows, acc_ref, m_ref, (s0, s1), seq)
    acc = acc_ref[...]
    o = acc[:HEAD_DIM] * (1.0 / acc[HEAD_DIM:HEAD_DIM + 1])
    for h in range(A_HEADS):
        o_ref[0, h * HEAD_DIM:(h + 1) * HEAD_DIM, :] = o[:, h * bq:(h + 1) * bq].astype(BF16)


def _attention_scratch(acc_rows, n):
    return [pltpu.VMEM((LANES, n), BF16), pltpu.VMEM((acc_rows, n), F32), pltpu.VMEM((1, n), F32),
            pltpu.VMEM((KV_CHUNK, n), F32), pltpu.VMEM((KV_CHUNK, n), F32)]


def _attention_a(qa, ka, va):
    bsz, _, s = qa.shape
    bq = A_Q_BLOCK
    assert s % (CHUNKS_PER_ITER * KV_CHUNK) == 0 and s % bq == 0
    n = A_HEADS * bq
    return pl.pallas_call(
        functools.partial(_attn_a_kernel, seq=s),
        grid=(bsz, s // bq),
        in_specs=[
            pl.BlockSpec((1, A_Q_W, bq), lambda b, i: (b, 0, i)),
            pl.BlockSpec((1, s, LANES), lambda b, i: (b, 0, 0)),
            pl.BlockSpec((1, A_KV_HEADS * VA_ROWS, s), lambda b, i: (b, 0, 0)),
        ],
        out_specs=pl.BlockSpec((1, A_Q_W, bq), lambda b, i: (b, 0, i)),
        out_shape=jax.ShapeDtypeStruct((bsz, A_Q_W, s), BF16),
        scratch_shapes=_attention_scratch(VA_ROWS, n),
        compiler_params=_cparams(("parallel", "parallel")),
        name="attention_a",
    )(qa, ka, va)


def _attn_b_kernel(lam_ref, q_ref, k_ref, v_ref, sg_ref, o_ref, rhs_ref, acc, m_ref, s0, s1, *, seq):
    bq = B_Q_BLOCK
    rhs_ref[...] = jnp.zeros_like(rhs_ref)
    rhs_ref[:HEAD_DIM, :bq] = q_ref[0, :HEAD_DIM, :]
    rhs_ref[HEAD_DIM:, bq:] = q_ref[0, HEAD_DIM:, :]
    _column_softmax_attention(k_ref.at[0, 0], rhs_ref, v_ref.at[0], lambda g: (0, VB_ROWS), acc, m_ref,
                              (s0, s1), seq)
    a = acc[...]
    o = a[:B_V_DIM] * (1.0 / a[B_V_DIM:B_V_DIM + 1])
    o = o[:, :bq] - lam_ref[0] * o[:, bq:]
    o = o * lax.rsqrt(jnp.mean(o * o, axis=0, keepdims=True) + EPS) * sg_ref[...]
    o_ref[0] = (o * (1.0 - LAMBDA_INIT)).astype(BF16)


def _attention_b(lam, qb, kb, vb, subln):
    bsz, _, s = qb.shape
    bq = B_Q_BLOCK
    assert s % (CHUNKS_PER_ITER * KV_CHUNK) == 0 and s % bq == 0
    n = 2 * bq
    return pl.pallas_call(
        functools.partial(_attn_b_kernel, seq=s),
        grid=(bsz, B_HEADS, s // bq),
        in_specs=[
            pl.BlockSpec(memory_space=pltpu.SMEM),
            pl.BlockSpec((1, B_V_DIM, bq), lambda b, h, i: (b, h, i)),
            pl.BlockSpec((1, 1, s, LANES), lambda b, h, i: (b, h, 0, 0)),
            pl.BlockSpec((1, VB_ROWS, s), lambda b, h, i: (b, h, 0)),
            pl.BlockSpec((B_V_DIM, bq), lambda b, h, i: (0, 0)),
        ],
        out_specs=pl.BlockSpec((1, B_V_DIM, bq), lambda b, h, i: (b, h, i)),
        out_shape=jax.ShapeDtypeStruct((bsz, B_V_W, s), BF16),
        scratch_shapes=_attention_scratch(VB_ROWS, n),
        compiler_params=_cparams(("parallel", "parallel", "parallel")),
        name="attention_b",
    )(lam, qb, kb, vb, subln)


def _store_token_tiles(ref, val):
    n = val.shape[0]
    for j in range(TILE_SUBLANES):
        ref[pl.ds(j, n, stride=TILE_SUBLANES), :] = val[:, j * LANES:(j + 1) * LANES]


def _load_token_tiles(ref, n):
    return jnp.concatenate([ref[pl.ds(j, n, stride=TILE_SUBLANES), :] for j in range(TILE_SUBLANES)], axis=1)


def _outproj_kernel(a_ref, b_ref, x_ref, wa_ref, wb_ref, g_ref, rw_ref, rb_ref,
                    x2_ref, xn_ref, te_ref, tg_ref):
    tn = (((0,), (0,)), ((), ()))
    y = (x_ref[0]
         + lax.dot_general(a_ref[0], wa_ref[...], tn, preferred_element_type=F32)
         + lax.dot_general(b_ref[0], wb_ref[...], tn, preferred_element_type=F32))
    x2_ref[0] = y
    xn = y * lax.rsqrt(jnp.mean(y * y, axis=-1, keepdims=True) + EPS) * g_ref[...]
    _store_token_tiles(xn_ref, xn)
    logits = lax.dot_general(rw_ref[...], xn, (((1,), (1,)), ((), ())),
                             precision=lax.Precision.HIGHEST, preferred_element_type=F32) + rb_ref[...]
    iota = lax.broadcasted_iota(jnp.int32, logits.shape, 0)
    work = logits
    vals, idxs = [], []
    for _ in range(TOP_K):
        mx = jnp.max(work, axis=0, keepdims=True)
        idx = jnp.min(jnp.where(work == mx, iota, N_EXPERTS), axis=0, keepdims=True)
        vals.append(mx)
        idxs.append(idx)
        work = jnp.where(iota == idx, -jnp.inf, work)
    ex = [jnp.exp(v - vals[0]) for v in vals]
    inv = 1.0 / (ex[0] + ex[1] + ex[2] + ex[3])
    te_ref[0] = jnp.concatenate(idxs, axis=0)
    tg_ref[0] = jnp.concatenate([e * inv for e in ex], axis=0)


def _out_projection(oa, ob, x, wa, wb, g_ffn, rw_t, rb):
    bsz, s, d = x.shape
    tm = TOKEN_TILE
    const = lambda shape: pl.BlockSpec(shape, lambda b, i: (0,) * len(shape))
    tok = pl.BlockSpec((1, tm, d), lambda b, i: (b, i, 0))
    sel = pl.BlockSpec((1, TOP_K, tm), lambda b, i: (b, 0, i))
    return pl.pallas_call(
        _outproj_kernel,
        grid=(bsz, s // tm),
        in_specs=[
            pl.BlockSpec((1, A_Q_W, tm), lambda b, i: (b, 0, i)),
            pl.BlockSpec((1, B_V_W, tm), lambda b, i: (b, 0, i)),
            tok,
            const((A_Q_W, d)),
            const((B_V_W, d)),
            const((1, d)),
            const((N_EXPERTS, d)),
            const((N_EXPERTS, tm)),
        ],
        out_specs=[tok,
                   pl.BlockSpec((tm * TILE_SUBLANES, LANES), lambda b, i: (b * (s // tm) + i, 0)),
                   sel, sel],
        out_shape=[
            jax.ShapeDtypeStruct((bsz, s, d), F32),
            jax.ShapeDtypeStruct((bsz * s * TILE_SUBLANES, LANES), F32),
            jax.ShapeDtypeStruct((bsz, TOP_K, s), jnp.int32),
            jax.ShapeDtypeStruct((bsz, TOP_K, s), F32),
        ],
        compiler_params=_cparams(("parallel", "parallel")),
        name="out_projection_router",
    )(oa, ob, x, wa, wb, g_ffn, rw_t, rb)


def _tile_rows(r):
    return pl.ds(r * TILE_SUBLANES, TILE_SUBLANES)


def _gather_row(idx_ref, r, src_hbm, buf, sem):
    src = pl.ds(pl.multiple_of(idx_ref[0, 0, r], TILE_SUBLANES), TILE_SUBLANES)
    return pltpu.make_async_copy(src_hbm.at[src, :], buf.at[_tile_rows(r), :], sem)


def _scatter_row(idx_ref, r, buf, dst_hbm, sem):
    dst = pl.ds(pl.multiple_of(idx_ref[0, 0, r], TILE_SUBLANES), TILE_SUBLANES)
    return pltpu.make_async_copy(buf.at[_tile_rows(r), :], dst_hbm.at[dst, :], sem)


def _copy_groups(rows, parts):
    first = rows // FIRST_GROUP_DIV
    rest = (rows - first) // (parts - 1)
    return [0] + [first + k * rest for k in range(parts - 1)] + [rows]


def _moe_kernel(be_ref, nb_ref, tok_first, tok_nxt, dst_prev, xn_hbm, wg_ref, bg_ref, wu_ref, bu_ref,
                wd_ref, bd_ref, y_hbm, xbuf_all, ybuf_all, gsem, ssem, *, pad_base):
    i = pl.program_id(0)
    n_used = nb_ref[0]
    rows = MOE_ROWS
    tile_rows = rows * TILE_SUBLANES
    xbufs = tuple(xbuf_all.at[pl.ds(s * tile_rows, tile_rows), :] for s in range(2))
    ybufs = tuple(ybuf_all.at[pl.ds(s * tile_rows, tile_rows), :] for s in range(2))

    @pl.when(i == 0)
    def _():
        ybuf_all[...] = jnp.zeros_like(ybuf_all)
        fills = [pltpu.make_async_copy(ybufs[0], y_hbm.at[pl.ds(pad_base + j * tile_rows, tile_rows), :],
                                       ssem.at[0]) for j in range(N_EXPERTS)]
        for f in fills:
            f.start()
        for f in fills:
            f.wait()
        fills[1].start()
        for r in range(rows):
            _gather_row(tok_first, r, xn_hbm, xbufs[0], gsem.at[0]).start(priority=r % 2)

    def step(slot):
        xbuf, ybuf = xbufs[slot], ybufs[slot]
        y_prev, sem_prev = ybufs[1 - slot], ssem.at[1 - slot]

        @pl.when(i <= n_used)
        def _():
            pltpu.make_async_copy(xn_hbm.at[pl.ds(0, tile_rows), :], xbuf, gsem.at[slot]).wait()

        def wait_scatter():
            pltpu.make_async_copy(ybuf, y_hbm.at[pl.ds(0, tile_rows), :], ssem.at[slot]).wait()

        pl.when(jnp.logical_and(i >= n_used, i - 2 < n_used))(wait_scatter)

        @pl.when(i == n_used)
        def _():
            for r in range(rows):
                _scatter_row(dst_prev, r, y_prev, y_hbm, sem_prev).start(priority=r % 2)

        @pl.when(i < n_used)
        def _():
            kb = _copy_groups(rows, K_SPLIT)
            kw = D_MODEL // K_SPLIT
            g = bg_ref[0]
            u = bu_ref[0]
            for q in range(K_SPLIT):
                for r in range(kb[q], kb[q + 1]):
                    _gather_row(tok_nxt, r, xn_hbm, xbufs[1 - slot], gsem.at[1 - slot]).start(priority=r % 2)
                xq = jnp.concatenate([xbuf[pl.ds(j, rows, stride=TILE_SUBLANES), :]
                                      for j in range(q * kw // LANES, (q + 1) * kw // LANES)], axis=1).astype(BF16)
                g = g + jnp.dot(xq, wg_ref[0, q * kw:(q + 1) * kw, :], preferred_element_type=F32)
                u = u + jnp.dot(xq, wu_ref[0, q * kw:(q + 1) * kw, :], preferred_element_type=F32)
            g = jnp.minimum(g, SWIGLU_LIMIT)
            u = jnp.clip(u, -SWIGLU_LIMIT, SWIGLU_LIMIT)
            hb = (g * jax.nn.sigmoid(g * SWIGLU_ALPHA) * (u + 1.0)).astype(BF16)
            wait_scatter()
            nb = _copy_groups(rows, N_SPLIT)
            nw = D_MODEL // N_SPLIT
            for q in range(N_SPLIT):
                for r in range(nb[q], nb[q + 1]):
                    _scatter_row(dst_prev, r, y_prev, y_hbm, sem_prev).start(priority=r % 2)
                cols = slice(q * nw, (q + 1) * nw)
                yq = jnp.dot(hb, wd_ref[0, :, cols], preferred_element_type=F32) + bd_ref[0, :, cols]
                for j in range(nw // LANES):
                    ybuf[pl.ds(q * nw // LANES + j, rows, stride=TILE_SUBLANES), :] = yq[:, j * LANES:(j + 1) * LANES]

    for slot in range(2):
        pl.when(lax.rem(i, 2) == slot)(functools.partial(step, slot))


def _moe_experts(block_e, n_used, src_tok, dst_row, n_out_rows, xn, wg, bg, wu, bu, wd, bd):
    d = D_MODEL
    t = xn.shape[0] // TILE_SUBLANES
    bm = MOE_ROWS
    n_blocks = src_tok.shape[0]
    dff = wg.shape[2]
    last = n_blocks - 1
    n_steps = n_blocks + 2
    blk = lambda i: jnp.minimum(i, last)
    wspec = lambda k, n: pl.BlockSpec((1, k, n), lambda i, be, nb: (be[blk(i)], 0, 0))
    idx = lambda fn: pl.BlockSpec((1, 1, bm), lambda i, be, nb: (fn(i), 0, 0), memory_space=pltpu.SMEM)
    grid_spec = pltpu.PrefetchScalarGridSpec(
        num_scalar_prefetch=2,
        grid=(n_steps,),
        in_specs=[
            idx(lambda i: 0), idx(lambda i: blk(i + 1)), idx(lambda i: jnp.minimum(i, n_blocks)),
            pl.BlockSpec(memory_space=pl.ANY),
            wspec(d, dff), wspec(1, dff), wspec(d, dff), wspec(1, dff), wspec(dff, d), wspec(1, d),
        ],
        out_specs=pl.BlockSpec(memory_space=pl.ANY),
        scratch_shapes=[pltpu.VMEM((2 * bm * TILE_SUBLANES, LANES), F32)] * 2
                       + [pltpu.SemaphoreType.DMA((2,)), pltpu.SemaphoreType.DMA((2,))],
    )
    return pl.pallas_call(
        functools.partial(_moe_kernel, pad_base=t * TOP_K * TILE_SUBLANES),
        grid_spec=grid_spec,
        out_shape=jax.ShapeDtypeStruct((n_out_rows * TILE_SUBLANES, LANES), F32),
        compiler_params=_cparams(("arbitrary",)),
        name="moe_experts",
    )(block_e, n_used, src_tok, src_tok, dst_row, xn, wg, bg, wu, bu, wd, bd)


def _combine_kernel(y0_ref, y1_ref, y2_ref, y3_ref, x2_ref, gate_ref, g_ref, o_ref):
    tc = x2_ref.shape[0]
    gates = gate_ref[...]
    y = x2_ref[...]
    for k, y_ref in enumerate((y0_ref, y1_ref, y2_ref, y3_ref)):
        y = y + gates[:, k:k + 1] * _load_token_tiles(y_ref, tc)
    o_ref[...] = y * lax.rsqrt(jnp.mean(y * y, axis=-1, keepdims=True) + EPS) * g_ref[...]


def _combine(y, x2, gates, g_final):
    t, d = x2.shape
    tc = COMBINE_TOKENS
    n = t // tc
    yspec = lambda k: pl.BlockSpec((tc * TILE_SUBLANES, LANES), lambda i: (k * n + i, 0))
    return pl.pallas_call(
        _combine_kernel,
        grid=(n,),
        in_specs=[
            yspec(0), yspec(1), yspec(2), yspec(3),
            pl.BlockSpec((tc, d), lambda i: (i, 0)),
            pl.BlockSpec((tc, TOP_K), lambda i: (i, 0)),
            pl.BlockSpec((1, d), lambda i: (0, 0)),
        ],
        out_specs=pl.BlockSpec((tc, d), lambda i: (i, 0)),
        out_shape=jax.ShapeDtypeStruct((t, d), F32),
        compiler_params=_cparams(("parallel",)),
        name="combine_final_norm",
    )(y, y, y, y, x2, gates, g_final)


def _route(top_e):
    t = top_e.shape[0]
    a = t * TOP_K
    bm = MOE_ROWS
    n_blocks = a // bm + N_EXPERTS
    flat_e = top_e.reshape(-1)
    order = jnp.argsort(flat_e, stable=True).astype(jnp.int32)
    counts = jnp.bincount(flat_e, length=N_EXPERTS).astype(jnp.int32)
    padded = ((counts + bm - 1) // bm) * bm
    start = jnp.cumsum(counts) - counts
    pend = jnp.cumsum(padded)
    pstart = pend - padded
    block_start = jnp.arange(n_blocks, dtype=jnp.int32) * bm
    block_e = jnp.minimum(jnp.sum((pend[None, :] <= block_start[:, None]).astype(jnp.int32), axis=1),
                          N_EXPERTS - 1)
    rows = jnp.arange(n_blocks * bm, dtype=jnp.int32)
    row_e = jnp.repeat(block_e, bm)
    within = rows - pstart[row_e]
    valid = within < counts[row_e]
    assign = order[jnp.clip(start[row_e] + within, 0, a - 1)]
    pad_before = jnp.cumsum(padded - counts) - (padded - counts)
    pad_rank = jnp.minimum(pad_before[row_e] + within - counts[row_e], N_EXPERTS * bm - 1)
    tok = assign // TOP_K
    src = (jnp.where(valid, tok, 0) * TILE_SUBLANES).astype(jnp.int32)
    dst = (jnp.where(valid, (assign % TOP_K) * t + tok, a + pad_rank) * TILE_SUBLANES).astype(jnp.int32)
    dummy = (a + jnp.arange(bm, dtype=jnp.int32)) * TILE_SUBLANES
    dst = jnp.concatenate([dummy, dst]).reshape(n_blocks + 1, 1, bm)
    n_used = (pend[-1] // bm).astype(jnp.int32).reshape(1)
    return block_e, n_used, src.reshape(n_blocks, 1, bm), dst, a + N_EXPERTS * bm


def _rope_tables(seq):
    rows = seq // GRID_W
    row = jnp.repeat(jnp.arange(rows, dtype=F32), GRID_W)
    col = jnp.tile(jnp.arange(GRID_W, dtype=F32), rows)
    half = HEAD_DIM // 2
    inv = AXIAL_THETA ** (-jnp.arange(0, half, 2, dtype=F32) / half)
    ang = jnp.concatenate([row[:, None] * inv, col[:, None] * inv], axis=-1)
    ang = jnp.concatenate([ang, ang], axis=-1).T
    sign_a = jnp.where(jnp.arange(HEAD_DIM) < half, -1.0, 1.0).astype(F32)[:, None]
    tt = jnp.arange(seq, dtype=F32)
    invb = ROPE_THETA ** (-jnp.arange(0, PARTIAL_ROT, 2, dtype=F32) / PARTIAL_ROT)
    angb = tt[:, None] * invb
    angb = jnp.concatenate([angb, angb], axis=-1).T
    sign_b = jnp.where(jnp.arange(PARTIAL_ROT) < PARTIAL_ROT // 2, -1.0, 1.0).astype(F32)[:, None]
    return jnp.cos(ang), jnp.sin(ang) * sign_a, jnp.cos(angb), jnp.sin(angb) * sign_b


def _trunk(x, p):
    bsz, s, d = x.shape
    t = bsz * s
    qa, qb, va, vb, ka, kb = _in_projection(x, p["g_attn"], p["w_in_t"], _rope_tables(s), p["qg"], p["kg"])
    oa = _attention_a(qa, ka, va)
    ob = _attention_b(p["lam"], qb, kb, vb, p["subln"])
    x2, xn, te, tg = _out_projection(oa, ob, x, p["wa"], p["wb"], p["g_ffn"], p["rw_t"], p["rb"])
    top_e = jnp.swapaxes(te, 1, 2).reshape(t, TOP_K)
    gates = jnp.swapaxes(tg, 1, 2).reshape(t, TOP_K)
    block_e, n_used, src_tok, dst_row, n_out_rows = _route(top_e)
    y = _moe_experts(block_e, n_used, src_tok, dst_row, n_out_rows, xn,
                     p["wg"], p["bg"], p["wu"], p["bu"], p["wd"], p["bd"])
    out = _combine(y, x2.reshape(t, d), gates, p["g_final"])
    return out.reshape(bsz, s, d)


def kernel(x_prompt, x_sample, attn_norm, w_in, a_q_norm, a_k_norm, b_lambda_q1, b_lambda_k1, b_lambda_q2,
           b_lambda_k2, b_subln, w_out, ffn_norm, router_w, router_b, w_gate, b_gate, w_up, b_up, w_down,
           b_down, final_norm):
    tm = TOKEN_TILE
    lam = (jnp.exp(jnp.sum(b_lambda_q1[0].astype(F32) * b_lambda_k1[0].astype(F32)))
           - jnp.exp(jnp.sum(b_lambda_q2[0].astype(F32) * b_lambda_k2[0].astype(F32)))
           + LAMBDA_INIT)
    p = {
        "g_attn": attn_norm[0].reshape(1, D_MODEL),
        "w_in_t": w_in[0].T.astype(BF16),
        "qg": jnp.broadcast_to(a_q_norm[0][:, None], (HEAD_DIM, tm)),
        "kg": jnp.broadcast_to(a_k_norm[0][:, None], (HEAD_DIM, tm)),
        "lam": lam.reshape(1).astype(F32),
        "subln": jnp.broadcast_to(b_subln[0][:, None], (B_V_DIM, B_Q_BLOCK)),
        "wa": w_out[0, :A_Q_W].astype(BF16),
        "wb": w_out[0, A_Q_W:].astype(BF16),
        "g_ffn": ffn_norm[0].reshape(1, D_MODEL),
        "rw_t": router_w[0].T,
        "rb": jnp.broadcast_to(router_b[0][:, None], (N_EXPERTS, tm)),
        "wg": w_gate[0].astype(BF16),
        "bg": b_gate[0].reshape(N_EXPERTS, 1, -1),
        "wu": w_up[0].astype(BF16),
        "bu": b_up[0].reshape(N_EXPERTS, 1, -1),
        "wd": w_down[0].astype(BF16),
        "bd": b_down[0].reshape(N_EXPERTS, 1, -1),
        "g_final": final_norm.reshape(1, D_MODEL),
    }
    return _trunk(x_prompt, p), _trunk(x_sample, p)
```
